```python
import math
import jax, jax.numpy as jnp
from jax import lax
import numpy as np

D_MODEL = 1024
BATCH = 8
SEQ = 4096
DEPTH = 2

EPS = 1e-6
NEG_BIG = -1e30
LB_FLOOR = 1e-20
A_HEADS = 4
A_HEAD_DIM = 64
A_KV_LATENT = 128
IDX_HEADS = 4
IDX_DIM = 64
MAX_TOPK = 256
Q_BLOCK = 128
N_BUCKETS = 32
MAX_DISTANCE = 128
B_HEADS = 4
B_DK = 64
B_DV = 64
B_GATE_RANK = 16
B_GATE_TAU = 16.0
C_WIDTH = 256
C_BLOCKS = 4
C_CONV = 4
C_EXP = 8.0
D_HEADS = 4
D_DK = 64
D_DV = 64
CHUNK = 64
N_BRANCH = 4
D_FF = 4 * D_MODEL

IN_SPLITS = (
    A_HEADS * A_HEAD_DIM, A_KV_LATENT, IDX_HEADS * IDX_DIM, IDX_DIM, IDX_HEADS,
    B_HEADS * B_DK, B_HEADS * B_DK, B_HEADS * B_DV, B_GATE_RANK, B_HEADS * B_DV,
    C_WIDTH, C_WIDTH,
    D_HEADS * D_DK, D_HEADS * D_DK, D_HEADS * D_DV, D_HEADS * D_DV,
)
IN_COLS = sum(IN_SPLITS)

kernel_name = 'hybrid_dsa_gla_rglru_hgrn2_block'


def rms_norm(x, g):
    xf = x.astype(jnp.float32)
    y = xf * lax.rsqrt(jnp.mean(xf * xf, axis=-1, keepdims=True) + EPS)
    return (y * g.astype(jnp.float32)).astype(x.dtype)


def split_heads(t, n):
    return t.reshape(t.shape[0], t.shape[1], n, -1)


def t5_bucket(dist):
    n = jnp.maximum(dist, 0)
    max_exact = N_BUCKETS // 2
    nf = jnp.maximum(n, max_exact).astype(jnp.float32)
    large = max_exact + (jnp.log(nf / max_exact) / math.log(MAX_DISTANCE / max_exact)
                         * (N_BUCKETS - max_exact)).astype(jnp.int32)
    large = jnp.minimum(large, N_BUCKETS - 1)
    return jnp.where(n < max_exact, n, large)


def dsa_sparse_attention(q, c_kv, iq, ik, iw, w_uk, w_uv, rel_bias):
    Bsz, T = q.shape[0], q.shape[1]
    topk = min(MAX_TOPK, T // 4)
    n_blk = T // Q_BLOCK
    q_lat = jnp.einsum('bthd,hdc->bthc', q, w_uk) * (A_HEAD_DIM ** -0.5)
    iw = iw * (IDX_HEADS ** -0.5 * IDX_DIM ** -0.5)
    key_pos = jnp.arange(T, dtype=jnp.int32)
    b_idx = jnp.arange(Bsz)[:, None, None]

    def to_blocks(a):
        return jnp.swapaxes(a.reshape(Bsz, n_blk, Q_BLOCK, *a.shape[2:]), 0, 1)

    def one_block(blk):
        ql, iqb, iwb, qpos = blk
        rel = jax.nn.relu(jnp.einsum('bqhd,bsd->bqhs', iqb, ik))
        score = jnp.einsum('bqhs,bqh->bqs', rel, iwb).astype(jnp.float32)
        score = jnp.where((key_pos[None, :] <= qpos[:, None])[None], score, NEG_BIG)
        _, idx = lax.top_k(score, topk)
        c_sel = c_kv[b_idx, idx]
        dist = qpos[None, :, None] - idx
        bias = jnp.swapaxes(rel_bias[t5_bucket(dist)], -1, -2).astype(jnp.float32)
        logits = jnp.einsum('bqhc,bqkc->bqhk', ql, c_sel).astype(jnp.float32) + bias
        logits = jnp.where((dist >= 0)[:, :, None, :], logits, NEG_BIG)
        p = jax.nn.softmax(logits, axis=-1).astype(c_sel.dtype)
        return jnp.einsum('bqhk,bqkc->bqhc', p, c_sel)

    o_lat = lax.map(one_block, (to_blocks(q_lat), to_blocks(iq), to_blocks(iw),
                                key_pos.reshape(n_blk, Q_BLOCK)))
    o_lat = jnp.swapaxes(o_lat, 0, 1).reshape(Bsz, T, A_HEADS, A_KV_LATENT)
    return jnp.einsum('bthc,hcd->bthd', o_lat, w_uv)


def chunked_gated_linear_attention(q, k, v, log_g):
    Bsz, T, H, dk = q.shape
    dv = v.shape[-1]
    n = T // CHUNK

    def to_chunks(a):
        return a.reshape(Bsz, n, CHUNK, H, a.shape[-1]).transpose(1, 0, 3, 2, 4)

    xs = (to_chunks(q), to_chunks(k), to_chunks(v), to_chunks(log_g.astype(jnp.float32)))
    causal = jnp.tril(jnp.ones((CHUNK, CHUNK), dtype=bool))[None, None, :, :, None]

    def step(S, inp):
        qi, ki, vi, gi = inp
        b = jnp.cumsum(gi, axis=2)
        b_last = b[:, :, -1, :]
        diff = b[:, :, :, None, :] - b[:, :, None, :, :]
        decay = jnp.where(causal, jnp.exp(jnp.where(causal, diff, 0.0)), 0.0)
        attn = jnp.einsum('bhid,bhjd,bhijd->bhij', qi, ki, decay)
        o = (jnp.einsum('bhij,bhjv->bhiv', attn, vi)
             + jnp.einsum('bhid,bhdv->bhiv', qi * jnp.exp(b), S))
        k_tail = ki * jnp.exp(b_last[:, :, None, :] - b)
        S_new = jnp.exp(b_last)[..., None] * S + jnp.einsum('bhjd,bhjv->bhdv', k_tail, vi)
        return S_new, o

    S0 = jnp.zeros((Bsz, H, dk, dv), jnp.float32)
    _, o = lax.scan(step, S0, xs)
    return o.transpose(1, 0, 3, 2, 4).reshape(Bsz, T, H, dv).astype(v.dtype)


def causal_depthwise_conv(x, w, b):
    T = x.shape[1]
    xp = jnp.pad(x, ((0, 0), (C_CONV - 1, 0), (0, 0)))
    y = xp[:, 0:T] * w[0]
    for j in range(1, C_CONV):
        y = y + xp[:, j:j + T] * w[j]
    return y + b


def block_diag_linear(x, w, b):
    xb = x.reshape(x.shape[0], x.shape[1], C_BLOCKS, -1)
    return jnp.einsum('btni,nij->btnj', xb, w).reshape(x.shape) + b


def rg_lru(x, w_a, b_a, w_x, b_x, lam):
    r = jax.nn.sigmoid(block_diag_linear(x, w_a, b_a)).astype(jnp.float32)
    i = jax.nn.sigmoid(block_diag_linear(x, w_x, b_x)).astype(jnp.float32)
    log_a = -C_EXP * r * jax.nn.softplus(-lam.astype(jnp.float32))
    a = jnp.exp(log_a)
    u = jnp.sqrt(jnp.maximum(-jnp.expm1(2.0 * log_a), 0.0)) * (i * x.astype(jnp.float32))

    def combine(c1, c2):
        a1, u1 = c1
        a2, u2 = c2
        return a1 * a2, a2 * u1 + u2

    _, h = lax.associative_scan(combine, (a, u), axis=1)
    return h.astype(x.dtype)


def setup_inputs(seed: int = 0) -> dict:
    key = jax.random.key(seed)
    ks = iter(jax.random.split(key, 40))
    f32 = jnp.float32
    L, D = DEPTH, D_MODEL
    bs = C_WIDTH // C_BLOCKS

    def nrm(shape, scale):
        return scale * jax.random.normal(next(ks), shape, f32)

    def gain(shape):
        return 1.0 + nrm(shape, 0.02)

    a_pow = jax.random.uniform(next(ks), (L, C_WIDTH), f32, 0.9, 0.999)
    a = a_pow ** (1.0 / C_EXP)
    lru_lambda = jnp.log(a) - jnp.log1p(-a)
    wa = A_HEADS * A_HEAD_DIM
    wb = B_HEADS * B_DV
    wd = D_HEADS * D_DV
    return {
        'x': nrm((BATCH, SEQ, D), 1.0),
        'norm1_g': gain((L, D)),
        'w_in': nrm((L, D, IN_COLS), D ** -0.5),
        'w_gate': nrm((L, D, N_BRANCH * D), D ** -0.5),
        'b_gate': nrm((L, N_BRANCH * D), 0.02),
        'kv_norm_g': gain((L, A_KV_LATENT)),
        'w_uk': nrm((L, A_HEADS, A_HEAD_DIM, A_KV_LATENT), A_HEAD_DIM ** -0.5),
        'w_uv': nrm((L, A_HEADS, A_KV_LATENT, A_HEAD_DIM), A_KV_LATENT ** -0.5),
        'rel_bias': nrm((N_BUCKETS, A_HEADS), 0.2),
        'w_gk2': nrm((L, B_GATE_RANK, B_HEADS * B_DK), B_GATE_RANK ** -0.5),
        'b_gk': nrm((L, B_HEADS * B_DK), 0.1),
        'gla_norm_g': gain((L, B_DV)),
        'conv_w': nrm((L, C_CONV, C_WIDTH), C_CONV ** -0.5),
        'conv_b': nrm((L, C_WIDTH), 0.02),
        'w_rg_a': nrm((L, C_BLOCKS, bs, bs), bs ** -0.5),
        'b_rg_a': nrm((L, C_WIDTH), 0.02),
        'w_rg_x': nrm((L, C_BLOCKS, bs, bs), bs ** -0.5),
        'b_rg_x': nrm((L, C_WIDTH), 0.02),
        'lru_lambda': lru_lambda,
        'lb_param': nrm((L, D_HEADS * D_DK), 0.1),
        'hgrn_norm_g': gain((L, D_DV)),
        'w_br_a': nrm((L, wa, D), wa ** -0.5),
        'w_br_b': nrm((L, wb, D), wb ** -0.5),
        'w_br_c': nrm((L, C_WIDTH, D), C_WIDTH ** -0.5),
        'w_br_d': nrm((L, wd, D), wd ** -0.5),
        'w_out': nrm((L, D, D), D ** -0.5),
        'norm2_g': gain((L, D)),
        'w_ff1': nrm((L, D, D_FF), D ** -0.5),
        'w_ff2': nrm((L, D_FF, D), D_FF ** -0.5),
        'final_norm_g': gain((D,)),
    }


def reference(x, norm1_g, w_in, w_gate, b_gate, kv_norm_g, w_uk, w_uv, rel_bias,
              w_gk2, b_gk, gla_norm_g, conv_w, conv_b, w_rg_a, b_rg_a, w_rg_x, b_rg_x,
              lru_lambda, lb_param, hgrn_norm_g, w_br_a, w_br_b, w_br_c, w_br_d,
              w_out, norm2_g, w_ff1, w_ff2, final_norm_g):
    Bsz, T, D = x.shape
    offsets = np.cumsum(IN_SPLITS)[:-1].tolist()
    lb_soft = jax.nn.softmax(lb_param.astype(jnp.float32), axis=0)
    lower_bounds = jnp.cumsum(lb_soft, axis=0) - lb_soft[0]
    for l in range(DEPTH):
        h = rms_norm(x, norm1_g[l])
        (a_q, a_ckv, a_iq, a_ik, a_iw,
         b_q, b_k, b_v, b_lr, b_r,
         c_x, c_y,
         d_q, d_f, d_i, d_g) = jnp.split(h @ w_in[l], offsets, axis=-1)

        y_a = dsa_sparse_attention(split_heads(a_q, A_HEADS), rms_norm(a_ckv, kv_norm_g[l]),
                                   split_heads(a_iq, IDX_HEADS), a_ik, a_iw,
                                   w_uk[l], w_uv[l], rel_bias).reshape(Bsz, T, -1)

        log_alpha = jax.nn.log_sigmoid((b_lr @ w_gk2[l] + b_gk[l]).astype(jnp.float32)) / B_GATE_TAU
        o_b = chunked_gated_linear_attention(split_heads(b_q, B_HEADS) * (B_DK ** -0.5),
                                             split_heads(b_k, B_HEADS), split_heads(b_v, B_HEADS),
                                             split_heads(log_alpha, B_HEADS))
        y_b = (rms_norm(o_b, gla_norm_g[l]) * jax.nn.silu(split_heads(b_r, B_HEADS))).reshape(Bsz, T, -1)

        xc = causal_depthwise_conv(c_x, conv_w[l], conv_b[l])
        y_c = rg_lru(xc, w_rg_a[l], b_rg_a[l], w_rg_x[l], b_rg_x[l], lru_lambda[l]) * jax.nn.gelu(c_y)

        lb = lower_bounds[l]
        log_f = jnp.logaddexp(jnp.log(jnp.maximum(lb, LB_FLOOR)),
                              jnp.log1p(-lb) + jax.nn.log_sigmoid(d_f.astype(jnp.float32)))
        o_d = chunked_gated_linear_attention(split_heads(jax.nn.silu(d_q), D_HEADS),
                                             split_heads(-jnp.expm1(log_f), D_HEADS),
                                             split_heads(d_i, D_HEADS), split_heads(log_f, D_HEADS))
        y_d = (rms_norm(o_d, hgrn_norm_g[l]) * jax.nn.silu(split_heads(d_g, D_HEADS))).reshape(Bsz, T, -1)

        gates = jax.nn.sigmoid(h @ w_gate[l] + b_gate[l]).reshape(Bsz, T, N_BRANCH, D)
        merged = (gates[:, :, 0] * (y_a @ w_br_a[l]) + gates[:, :, 1] * (y_b @ w_br_b[l])
                  + gates[:, :, 2] * (y_c @ w_br_c[l]) + gates[:, :, 3] * (y_d @ w_br_d[l]))
        x = x + merged @ w_out[l]

        h2 = rms_norm(x, norm2_g[l])
        x = x + jnp.square(jax.nn.relu(h2 @ w_ff1[l])) @ w_ff2[l]
    return rms_norm(x, final_norm_g)
```

```python
import functools
import math

import numpy as np
import jax
import jax.numpy as jnp
from jax import lax
from jax.experimental import pallas as pl
from jax.experimental.pallas import tpu as pltpu

F32 = jnp.float32
BF16 = jnp.bfloat16
I32 = jnp.int32

EPS = 1e-6
NEG_BIG = -1e30
LB_FLOOR = 1e-20
INT_MIN = -(2 ** 31)

A_HEADS = 4
A_HEAD_DIM = 64
A_KV_LATENT = 128
IDX_HEADS = 4
IDX_DIM = 64
MAX_TOPK = 256
Q_BLOCK = 128
N_BUCKETS = 32
MAX_DISTANCE = 128
B_HEADS = 4
B_DK = 64
B_DV = 64
B_GATE_RANK = 16
B_GATE_TAU = 16.0
C_WIDTH = 256
C_BLOCKS = 4
C_CONV = 4
C_EXP = 8.0
D_HEADS = 4
D_DK = 64
D_DV = 64
CHUNK = 64
N_BRANCH = 4

HEAD_W = 256
LANES = 128

IN_SPLITS = (
    A_HEADS * A_HEAD_DIM, A_KV_LATENT, IDX_HEADS * IDX_DIM, IDX_DIM, IDX_HEADS,
    B_HEADS * B_DK, B_HEADS * B_DK, B_HEADS * B_DV, B_GATE_RANK, B_HEADS * B_DV,
    C_WIDTH, C_WIDTH,
    D_HEADS * D_DK, D_HEADS * D_DK, D_HEADS * D_DV, D_HEADS * D_DV,
)

AQ_W = 640
AK_W = 384
PB_W = 1152
PC_W = 512
PD_W = 1024
IN_W = AQ_W + AK_W + PB_W + PC_W + PD_W

VMEM_LIMIT = 56 * 1024 * 1024

TM_PROJ = 512
TT_GLA = 256
TT_LRU = 256
GLA_LEVELS = (32, 16, 8, 4, 2, 1)


def _rms(x, g):
    return x * lax.rsqrt(jnp.mean(x * x, axis=-1, keepdims=True) + EPS) * g


def _dot(a, b):
    return jnp.dot(a, b, preferred_element_type=F32)


def _dot_nt(a, b):
    return lax.dot_general(a, b, (((1,), (1,)), ((), ())), preferred_element_type=F32)


def _stack_heads(a, lane_head):
    return jnp.concatenate([jnp.where(lane_head == h, a, 0.0) for h in range(4)], axis=0)


def _params(*sem):
    return pltpu.CompilerParams(dimension_semantics=sem, vmem_limit_bytes=VMEM_LIMIT)


def _const_spec(shape):
    nd = len(shape)
    return pl.BlockSpec(shape, lambda *_: (0,) * nd)


def _inproj_kernel(x_ref, g_ref, w_ref, kvg_ref, aq_ref, ak_ref, pb_ref, pc_ref, pd_ref):
    h = _rms(x_ref[...], g_ref[...]).astype(BF16)
    o = 0
    aq_ref[...] = _dot(h, w_ref[:, o:o + AQ_W])
    o += AQ_W
    ak = _dot(h, w_ref[:, o:o + AK_W])
    ak_ref[:, :A_KV_LATENT] = _rms(ak[:, :A_KV_LATENT], kvg_ref[...]).astype(BF16)
    ak_ref[:, A_KV_LATENT:] = ak[:, A_KV_LATENT:].astype(BF16)
    o += AK_W
    pb_ref[...] = _dot(h, w_ref[:, o:o + PB_W])
    o += PB_W
    pc_ref[...] = _dot(h, w_ref[:, o:o + PC_W])
    o += PC_W
    pd_ref[...] = _dot(h, w_ref[:, o:o + PD_W])


def _inproj(x2, g, w, kvg):
    n, d = x2.shape
    tm = min(TM_PROJ, n)
    widths = (AQ_W, AK_W, PB_W, PC_W, PD_W)
    dtypes = (F32, BF16, F32, F32, F32)
    return pl.pallas_call(
        _inproj_kernel,
        grid=(n // tm,),
        in_specs=[pl.BlockSpec((tm, d), lambda i: (i, 0)), _const_spec((1, d)),
                  _const_spec((d, IN_W)), _const_spec((1, A_KV_LATENT))],
        out_specs=[pl.BlockSpec((tm, wd), lambda i: (i, 0)) for wd in widths],
        out_shape=[jax.ShapeDtypeStruct((n, wd), dt) for wd, dt in zip(widths, dtypes)],
        compiler_params=_params("parallel"),
        name="inproj",
    )(x2, g, w, kvg)


def _t5_bucket(dist):
    n = jnp.maximum(dist, 0)
    max_exact = N_BUCKETS // 2
    nf = jnp.maximum(n, max_exact).astype(F32)
    large = max_exact + (jnp.log(nf / max_exact) / math.log(MAX_DISTANCE / max_exact)
                         * (N_BUCKETS - max_exact)).astype(I32)
    large = jnp.minimum(large, N_BUCKETS - 1)
    return jnp.where(n < max_exact, n, large)


def _bias_tiles_kernel(rb_ref, out_ref):
    row = lax.broadcasted_iota(I32, (Q_BLOCK, Q_BLOCK), 0)
    col = lax.broadcasted_iota(I32, (Q_BLOCK, Q_BLOCK), 1)
    for d in range(3):
        bucket = _t5_bucket(d * Q_BLOCK + row - col)
        for h in range(A_HEADS):
            acc = jnp.zeros((Q_BLOCK, Q_BLOCK), F32)
            for k in range(N_BUCKETS):
                acc = jnp.where(bucket == k, rb_ref[k, h], acc)
            out_ref[d, h * Q_BLOCK:(h + 1) * Q_BLOCK, :] = acc


def _bias_tiles(rel_bias):
    return pl.pallas_call(
        _bias_tiles_kernel,
        in_specs=[pl.BlockSpec(memory_space=pltpu.SMEM)],
        out_specs=pl.BlockSpec(memory_space=pltpu.VMEM),
        out_shape=jax.ShapeDtypeStruct((3, A_HEADS * Q_BLOCK, Q_BLOCK), F32),
        name="bias_tiles",
    )(rel_bias)


def _dsa_kernel(aq_ref, ak_ref, wuk_ref, wuv_ref, bias_ref, out_ref, keys_ref, *, topk):
    qb = Q_BLOCK
    qi = pl.program_id(1)
    nkb = qi + 1
    aq = aq_ref[...]
    lane_head = lax.broadcasted_iota(I32, (qb, HEAD_W), 1) // A_HEAD_DIM
    row = lax.broadcasted_iota(I32, (qb, qb), 0)
    col = lax.broadcasted_iota(I32, (qb, qb), 1)
    qpos = qi * qb + row

    qx = _stack_heads(aq[:, 256:512], lane_head).astype(BF16)
    qlat = _dot(_stack_heads(aq[:, 0:256], lane_head).astype(BF16), wuk_ref[...])
    qlat = (qlat * (A_HEAD_DIM ** -0.5)).astype(BF16)
    iw = aq[:, 512:512 + IDX_HEADS] * (IDX_HEADS ** -0.5 * IDX_DIM ** -0.5)
    iw_b = [jnp.broadcast_to(iw[:, h:h + 1], (qb, qb)) for h in range(IDX_HEADS)]

    def score_body(kb, _):
        off = pl.multiple_of(kb * qb, qb)
        ik = ak_ref[pl.ds(off, qb), A_KV_LATENT:]
        rel = jnp.maximum(_dot_nt(qx, ik), 0.0)
        sc = jnp.zeros((qb, qb), F32)
        for h in range(IDX_HEADS):
            sc = sc + rel[h * qb:(h + 1) * qb] * iw_b[h]
        sc = jnp.where(kb * qb + col <= qpos, sc, NEG_BIG)
        bits = pltpu.bitcast(sc, I32)
        keys_ref[kb] = bits ^ ((bits >> 31) & 0x7FFFFFFF)
        return 0

    lax.fori_loop(0, nkb, score_body, 0)

    def count(pred_fn):
        def body(kb, acc):
            return acc + jnp.where(pred_fn(keys_ref[kb]), 1.0, 0.0)
        acc = lax.fori_loop(0, nkb, body, jnp.zeros((qb, qb), F32))
        return jnp.sum(acc, axis=1, keepdims=True)

    def bit_body(i, theta):
        cand = theta + jnp.left_shift(jnp.int32(1), 31 - i)
        cand_b = jnp.broadcast_to(cand, (qb, qb))
        cnt = count(lambda key: key >= cand_b)
        return jnp.where(cnt >= topk, cand, theta)

    theta = lax.fori_loop(0, 32, bit_body, jnp.full((qb, 1), INT_MIN, I32))
    theta_b = jnp.broadcast_to(theta, (qb, qb))
    need = topk - count(lambda key: key > theta_b)

    tri = jnp.where(row < col, 1.0, 0.0).astype(BF16)

    def att_body(kb, carry):
        m, l, acc, tie_seen = carry
        off = pl.multiple_of(kb * qb, qb)
        ck = ak_ref[pl.ds(off, qb), :A_KV_LATENT]
        s = _dot_nt(qlat, ck)
        bias = bias_ref[jnp.minimum(qi - kb, 2)]
        key = keys_ref[kb]
        eq = key == theta_b
        eqf = jnp.where(eq, 1.0, 0.0)
        rank = tie_seen + _dot(eqf.astype(BF16), tri)
        sel = (key > theta_b) | (eq & (rank < need))
        valid = sel & (kb * qb + col <= qpos)
        tie_seen = tie_seen + jnp.sum(eqf, axis=1, keepdims=True)
        ms, ls, ps, alphas = [], [], [], []
        for h in range(A_HEADS):
            sl = slice(h * qb, (h + 1) * qb)
            sh = jnp.where(valid, s[sl] + bias[sl], NEG_BIG)
            mn = jnp.maximum(m[sl], jnp.max(sh, axis=1, keepdims=True))
            alpha = jnp.exp(m[sl] - mn)
            p = jnp.where(valid, jnp.exp(sh - mn), 0.0)
            ms.append(mn)
            ls.append(alpha * l[sl] + jnp.sum(p, axis=1, keepdims=True))
            ps.append(p)
            alphas.append(alpha)
        p = jnp.concatenate(ps, axis=0).astype(BF16)
        acc = jnp.concatenate(alphas, axis=0) * acc + _dot(p, ck)
        return jnp.concatenate(ms, axis=0), jnp.concatenate(ls, axis=0), acc, tie_seen

    nrow = A_HEADS * qb
    init = (jnp.full((nrow, 1), NEG_BIG, F32), jnp.zeros((nrow, 1), F32),
            jnp.zeros((nrow, A_KV_LATENT), F32), jnp.zeros((qb, 1), F32))
    _, l, acc, _ = lax.fori_loop(0, nkb, att_body, init)
    o = (acc / l).astype(BF16)
    y = jnp.zeros((qb, HEAD_W), F32)
    for h in range(A_HEADS):
        y = y + _dot(o[h * qb:(h + 1) * qb], wuv_ref[h])
    out_ref[...] = y


def _dsa(aq, ak, wuk, wuv, bias_tiles):
    b, t, _ = aq.shape
    nq = t // Q_BLOCK
    topk = min(MAX_TOPK, t // 4)
    return pl.pallas_call(
        functools.partial(_dsa_kernel, topk=topk),
        grid=(b, nq),
        in_specs=[pl.BlockSpec((None, Q_BLOCK, AQ_W), lambda i, j: (i, j, 0)),
                  pl.BlockSpec((None, t, AK_W), lambda i, j: (i, 0, 0)),
                  _const_spec(wuk.shape), _const_spec(wuv.shape), _const_spec(bias_tiles.shape)],
        out_specs=pl.BlockSpec((None, Q_BLOCK, HEAD_W), lambda i, j: (i, j, 0)),
        out_shape=jax.ShapeDtypeStruct((b, t, HEAD_W), F32),
        scratch_shapes=[pltpu.VMEM((nq, Q_BLOCK, Q_BLOCK), I32)],
        compiler_params=_params("parallel", "arbitrary"),
        name="dsa",
    )(aq, ak, wuk, wuv, bias_tiles)


def _gla_static_tables():
    c = CHUNK
    idx = np.arange(c)
    tri = (idx[None, :] <= idx[:, None]).astype(np.float32)
    blocks = [tri]
    for s in GLA_LEVELS:
        end_left = (idx // (2 * s)) * 2 * s + s - 1
        blocks.append(tri[end_left])
    blocks.append(np.ones((c, c), np.float32))
    lev = np.full((c, c), -1, np.int32)
    for i in range(c):
        for j in range(c):
            if i == j:
                lev[i, j] = len(GLA_LEVELS)
            elif i > j:
                s = 1 << int(math.floor(math.log2(i ^ j)))
                lev[i, j] = GLA_LEVELS.index(s)
    pool = np.kron(np.eye(4, dtype=np.float32), np.full((64, 64), 1.0 / 64, np.float32))
    return np.concatenate(blocks, axis=0), np.tile(lev, (1, 4)), pool


def _expm1(x):
    return jnp.tanh(0.5 * x) * (jnp.exp(x) + 1.0)


def _log_sigmoid(x):
    return -(jnp.maximum(-x, 0.0) + jnp.log1p(jnp.exp(-jnp.abs(x))))


def _gla_core(q, k, v, g, mall, lev, st):
    c = CHUNK
    lane_head = lax.broadcasted_iota(I32, (c, HEAD_W), 1) // 64
    rowc = lax.broadcasted_iota(I32, (c, HEAD_W), 0)
    bc = jnp.dot(mall, g, preferred_element_type=F32, precision=lax.Precision.HIGHEST)
    b = bc[0:c]
    blast = bc[(len(GLA_LEVELS) + 1) * c:]
    att = jnp.zeros((c, HEAD_W), F32)
    for li, s in enumerate(GLA_LEVELS):
        mid = bc[(li + 1) * c:(li + 2) * c]
        right = (rowc & s) != 0
        qt = jnp.where(right, q * jnp.exp(jnp.minimum(b - mid, 0.0)), 0.0)
        kt = jnp.where(right, 0.0, k * jnp.exp(jnp.minimum(mid - b, 0.0)))
        al = _dot_nt(qt.astype(BF16), _stack_heads(kt, lane_head).astype(BF16))
        att = jnp.where(lev == li, al, att)
    ad = _dot_nt(q.astype(BF16), _stack_heads(k, lane_head).astype(BF16))
    att = jnp.where(lev == len(GLA_LEVELS), ad, att)
    o = _dot(att.astype(BF16), _stack_heads(v, lane_head).astype(BF16))
    o = o + _dot_nt((q * jnp.exp(b)).astype(BF16), st.astype(BF16))
    ktail = k * jnp.exp(blast - b)
    upd = _dot(v.T.astype(BF16), ktail.astype(BF16))
    r2 = lax.broadcasted_iota(I32, (HEAD_W, HEAD_W), 0) // 64
    c2 = lax.broadcasted_iota(I32, (HEAD_W, HEAD_W), 1) // 64
    st = st * jnp.exp(blast[0:1, :]) + jnp.where(r2 == c2, upd, 0.0)
    return o, st


def _gla_finish(o, r, ng, pool):
    ms = jnp.dot(o * o, pool, preferred_element_type=F32, precision=lax.Precision.HIGHEST)
    return o * lax.rsqrt(ms + EPS) * ng * jax.nn.silu(r)


def _gla_b_kernel(p_ref, wgk_ref, bgk_ref, ng_ref, mall_ref, lev_ref, pool_ref, out_ref, st_ref):
    @pl.when(pl.program_id(1) == 0)
    def _():
        st_ref[...] = jnp.zeros_like(st_ref)

    tile = p_ref[...]
    q = tile[:, 0:256] * (B_DK ** -0.5)
    k = tile[:, 256:512]
    v = tile[:, 512:768]
    r = tile[:, 768:1024]
    z = _dot(tile[:, 1024:1152].astype(BF16), wgk_ref[...]) + bgk_ref[...]
    g = _log_sigmoid(z) / B_GATE_TAU
    mall, lev, st = mall_ref[...], lev_ref[...], st_ref[...]
    outs = []
    for c in range(tile.shape[0] // CHUNK):
        sl = slice(c * CHUNK, (c + 1) * CHUNK)
        o, st = _gla_core(q[sl], k[sl], v[sl], g[sl], mall, lev, st)
        outs.append(o)
    st_ref[...] = st
    out_ref[...] = _gla_finish(jnp.concatenate(outs, axis=0), r, ng_ref[...], pool_ref[...])


def _gla_d_kernel(p_ref, lbp_ref, ng_ref, mall_ref, lev_ref, pool_ref, out_ref, st_ref, *, layer):
    @pl.when(pl.program_id(1) == 0)
    def _():
        st_ref[...] = jnp.zeros_like(st_ref)

    lbp = lbp_ref[...]
    e = jnp.exp(lbp - jnp.max(lbp, axis=0, keepdims=True))
    soft = e / jnp.sum(e, axis=0, keepdims=True)
    cum = soft[0:1]
    for i in range(1, layer + 1):
        cum = cum + soft[i:i + 1]
    lb = cum - soft[0:1]

    tile = p_ref[...]
    q = jax.nn.silu(tile[:, 0:256])
    v = tile[:, 512:768]
    r = tile[:, 768:1024]
    t1 = jnp.log(jnp.maximum(lb, LB_FLOOR))
    t2 = jnp.log1p(-lb) + _log_sigmoid(tile[:, 256:512])
    g = jnp.maximum(t1, t2) + jnp.log1p(jnp.exp(-jnp.abs(t1 - t2)))
    k = -_expm1(g)
    mall, lev, st = mall_ref[...], lev_ref[...], st_ref[...]
    outs = []
    for c in range(tile.shape[0] // CHUNK):
        sl = slice(c * CHUNK, (c + 1) * CHUNK)
        o, st = _gla_core(q[sl], k[sl], v[sl], g[sl], mall, lev, st)
        outs.append(o)
    st_ref[...] = st
    out_ref[...] = _gla_finish(jnp.concatenate(outs, axis=0), r, ng_ref[...], pool_ref[...])


def _gla_call(kernel, p, extra, name):
    b, t, w = p.shape
    tt = min(TT_GLA, t)
    mall, lev, pool = _gla_static_tables()
    consts = list(extra) + [jnp.asarray(mall), jnp.asarray(lev), jnp.asarray(pool)]
    return pl.pallas_call(
        kernel,
        grid=(b, t // tt),
        in_specs=[pl.BlockSpec((None, tt, w), lambda i, j: (i, j, 0))]
        + [_const_spec(c.shape) for c in consts],
        out_specs=pl.BlockSpec((None, tt, HEAD_W), lambda i, j: (i, j, 0)),
        out_shape=jax.ShapeDtypeStruct((b, t, HEAD_W), F32),
        scratch_shapes=[pltpu.VMEM((HEAD_W, HEAD_W), F32)],
        compiler_params=_params("parallel", "arbitrary"),
        name=name,
    )(p, *consts)


def _lru_kernel(p_ref, cw_ref, cb_ref, wa_ref, ba_ref, wx_ref, bx_ref, lam_ref, out_ref,
                xpad_ref, h_ref):
    tt = p_ref.shape[0]
    pad = 8

    @pl.when(pl.program_id(1) == 0)
    def _():
        xpad_ref[0:pad, :] = jnp.zeros((pad, C_WIDTH), F32)
        h_ref[...] = jnp.zeros_like(h_ref)

    x = p_ref[:, 0:C_WIDTH]
    xpad_ref[pad:pad + tt, :] = x
    base = pad - (C_CONV - 1)
    xc = xpad_ref[base:base + tt, :] * cw_ref[0:1, :]
    for j in range(1, C_CONV):
        xc = xc + xpad_ref[base + j:base + j + tt, :] * cw_ref[j:j + 1, :]
    xc = xc + cb_ref[...]
    xpad_ref[0:pad, :] = x[tt - pad:tt, :]

    xb = xc.astype(BF16)
    r = jax.nn.sigmoid(_dot(xb, wa_ref[...]) + ba_ref[...])
    i = jax.nn.sigmoid(_dot(xb, wx_ref[...]) + bx_ref[...])
    lam = lam_ref[...]
    softplus_neg_lam = jnp.maximum(-lam, 0.0) + jnp.log1p(jnp.exp(-jnp.abs(lam)))
    log_a = -C_EXP * r * softplus_neg_lam
    a = jnp.exp(log_a)
    u = jnp.sqrt(jnp.maximum(-_expm1(2.0 * log_a), 0.0)) * (i * xc)

    rows = lax.broadcasted_iota(I32, (tt, C_WIDTH), 0)
    s = 1
    while s < tt:
        keep = rows >= s
        u = jnp.where(keep, a * pltpu.roll(u, s, 0) + u, u)
        a = jnp.where(keep, a * pltpu.roll(a, s, 0), a)
        s *= 2
    h = u + a * h_ref[0:1, :]
    h_ref[...] = jnp.broadcast_to(h[tt - 1:tt, :], h_ref.shape)
    out_ref[...] = h * jax.nn.gelu(p_ref[:, C_WIDTH:2 * C_WIDTH])


def _lru(pc, cw, cb, wa, ba, wx, bx, lam):
    b, t, w = pc.shape
    tt = min(TT_LRU, t)
    consts = [cw, cb, wa, ba, wx, bx, lam]
    return pl.pallas_call(
        _lru_kernel,
        grid=(b, t // tt),
        in_specs=[pl.BlockSpec((None, tt, w), lambda i, j: (i, j, 0))]
        + [_const_spec(c.shape) for c in consts],
        out_specs=pl.BlockSpec((None, tt, C_WIDTH), lambda i, j: (i, j, 0)),
        out_shape=jax.ShapeDtypeStruct((b, t, C_WIDTH), F32),
        scratch_shapes=[pltpu.VMEM((tt + 8, C_WIDTH), F32), pltpu.VMEM((8, C_WIDTH), F32)],
        compiler_params=_params("parallel", "arbitrary"),
        name="rglru",
    )(pc, *consts)


def _merge_kernel(x_ref, ya_ref, yb_ref, yc_ref, yd_ref, g_ref, wg_ref, bg_ref, wbr_ref, wo_ref,
                  out_ref):
    x = x_ref[...]
    d = x.shape[1]
    h = _rms(x, g_ref[...]).astype(BF16)
    merged = jnp.zeros(x.shape, F32)
    for n, y_ref in enumerate((ya_ref, yb_ref, yc_ref, yd_ref)):
        gate = jax.nn.sigmoid(_dot(h, wg_ref[:, n * d:(n + 1) * d]) + bg_ref[:, n * d:(n + 1) * d])
        merged = merged + gate * _dot(y_ref[...].astype(BF16), wbr_ref[n])
    out_ref[...] = x + _dot(merged.astype(BF16), wo_ref[...])


def _merge(x2, ys, g, wg, bg, wbr, wo):
    n, d = x2.shape
    tm = min(TM_PROJ, n)
    consts = [g, wg, bg, wbr, wo]
    return pl.pallas_call(
        _merge_kernel,
        grid=(n // tm,),
        in_specs=[pl.BlockSpec((tm, d), lambda i: (i, 0))]
        + [pl.BlockSpec((tm, HEAD_W), lambda i: (i, 0)) for _ in ys]
        + [_const_spec(c.shape) for c in consts],
        out_specs=pl.BlockSpec((tm, d), lambda i: (i, 0)),
        out_shape=jax.ShapeDtypeStruct((n, d), F32),
        compiler_params=_params("parallel"),
        name="merge",
    )(x2, *ys, *consts)


def _ffn_kernel(x_ref, g_ref, w1_ref, w2_ref, gf_ref, out_ref, *, final):
    x = x_ref[...]
    d = x.shape[1]
    h = _rms(x, g_ref[...]).astype(BF16)
    acc = x
    for c in range(w1_ref.shape[1] // d):
        hid = jnp.square(jnp.maximum(_dot(h, w1_ref[:, c * d:(c + 1) * d]), 0.0))
        acc = acc + _dot(hid.astype(BF16), w2_ref[c * d:(c + 1) * d, :])
    if final:
        acc = _rms(acc, gf_ref[...])
    out_ref[...] = acc


def _ffn(x2, g, w1, w2, gf, final):
    n, d = x2.shape
    tm = min(TM_PROJ, n)
    consts = [g, w1, w2, gf]
    return pl.pallas_call(
        functools.partial(_ffn_kernel, final=final),
        grid=(n // tm,),
        in_specs=[pl.BlockSpec((tm, d), lambda i: (i, 0))] + [_const_spec(c.shape) for c in consts],
        out_specs=pl.BlockSpec((tm, d), lambda i: (i, 0)),
        out_shape=jax.ShapeDtypeStruct((n, d), F32),
        compiler_params=_params("parallel"),
        name="ffn",
    )(x2, *consts)


def _permute_w_in(w):
    offs = np.concatenate([[0], np.cumsum(IN_SPLITS)])
    col = lambda i: w[:, offs[i]:offs[i + 1]]
    zeros = lambda n: jnp.zeros((w.shape[0], n), w.dtype)
    (a_q, a_ckv, a_iq, a_ik, a_iw, b_q, b_k, b_v, b_lr, b_r, c_x, c_y, d_q, d_f, d_i, d_g) = (
        col(i) for i in range(len(IN_SPLITS)))
    parts = [a_q, a_iq, a_iw, zeros(LANES - IDX_HEADS),
             a_ckv, a_ik, a_ik, a_ik, a_ik,
             b_q, b_k, b_v, b_r, b_lr, zeros(LANES - B_GATE_RANK),
             c_x, c_y, d_q, d_f, d_i, d_g]
    return jnp.concatenate(parts, axis=1).astype(BF16)


def _block_diag(w):
    n, bi, bj = w.shape
    eye = jnp.eye(n, dtype=w.dtype)
    return (w[:, :, None, :] * eye[:, None, :, None]).reshape(n * bi, n * bj)


def _wide_heads(w):
    h, c, d = w.shape
    eye = jnp.eye(h, dtype=w.dtype)
    return (w[:, :, None, :] * eye[:, None, :, None]).reshape(h, c, h * d)


def kernel(x, norm1_g, w_in, w_gate, b_gate, kv_norm_g, w_uk, w_uv, rel_bias, w_gk2, b_gk, gla_norm_g, conv_w, conv_b, w_rg_a, b_rg_a, w_rg_x, b_rg_x, lru_lambda, lb_param, hgrn_norm_g, w_br_a, w_br_b, w_br_c, w_br_d, w_out, norm2_g, w_ff1, w_ff2, final_norm_g):
    bsz, t, d = x.shape
    depth = w_in.shape[0]
    n = bsz * t
    row = lambda v: v.reshape(1, -1)
    bias_tiles = _bias_tiles(rel_bias)
    x2 = x.reshape(n, d)
    for l in range(depth):
        aq, ak, pb, pc, pd = _inproj(x2, row(norm1_g[l]), _permute_w_in(w_in[l]), row(kv_norm_g[l]))
        to3 = lambda a: a.reshape(bsz, t, a.shape[-1])

        y_a = _dsa(to3(aq), to3(ak), w_uk[l].reshape(A_HEADS * A_HEAD_DIM, A_KV_LATENT).astype(BF16),
                   _wide_heads(w_uv[l]).astype(BF16), bias_tiles)

        wgk = jnp.concatenate([w_gk2[l], jnp.zeros((LANES - B_GATE_RANK, HEAD_W), F32)], axis=0)
        y_b = _gla_call(_gla_b_kernel, to3(pb),
                        [wgk.astype(BF16), row(b_gk[l]), row(jnp.tile(gla_norm_g[l], B_HEADS))], "gla")

        y_c = _lru(to3(pc), conv_w[l], row(conv_b[l]), _block_diag(w_rg_a[l]).astype(BF16),
                   row(b_rg_a[l]), _block_diag(w_rg_x[l]).astype(BF16), row(b_rg_x[l]),
                   row(lru_lambda[l]))

        y_d = _gla_call(functools.partial(_gla_d_kernel, layer=l), to3(pd),
                        [lb_param, row(jnp.tile(hgrn_norm_g[l], D_HEADS))], "hgrn2")

        ys = [y.reshape(n, HEAD_W) for y in (y_a, y_b, y_c, y_d)]
        wbr = jnp.stack([w_br_a[l], w_br_b[l], w_br_c[l], w_br_d[l]]).astype(BF16)
        x2 = _merge(x2, ys, row(norm1_g[l]), w_gate[l].astype(BF16), row(b_gate[l]), wbr,
                    w_out[l].astype(BF16))
        x2 = _ffn(x2, row(norm2_g[l]), w_ff1[l].astype(BF16), w_ff2[l].astype(BF16),
                  row(final_norm_g), final=(l == depth - 1))
    return x2.reshape(bsz, t, d)
```

```python
import functools
import math

import numpy as np
import jax
import jax.numpy as jnp
from jax import lax
from jax.experimental import pallas as pl
from jax.experimental.pallas import tpu as pltpu

F32 = jnp.float32
BF16 = jnp.bfloat16
I32 = jnp.int32

EPS = 1e-6
NEG_BIG = -1e30
LB_FLOOR = 1e-20
INT_MIN = -(2 ** 31)

A_HEADS = 4
A_HEAD_DIM = 64
A_KV_LATENT = 128
IDX_HEADS = 4
IDX_DIM = 64
MAX_TOPK = 256
Q_BLOCK = 128
N_BUCKETS = 32
MAX_DISTANCE = 128
B_HEADS = 4
B_DK = 64
B_DV = 64
B_GATE_RANK = 16
B_GATE_TAU = 16.0
C_WIDTH = 256
C_BLOCKS = 4
C_CONV = 4
C_EXP = 8.0
D_HEADS = 4
D_DK = 64
D_DV = 64
CHUNK = 64
N_BRANCH = 4

HEAD_W = 256
LANES = 128

IN_SPLITS = (
    A_HEADS * A_HEAD_DIM, A_KV_LATENT, IDX_HEADS * IDX_DIM, IDX_DIM, IDX_HEADS,
    B_HEADS * B_DK, B_HEADS * B_DK, B_HEADS * B_DV, B_GATE_RANK, B_HEADS * B_DV,
    C_WIDTH, C_WIDTH,
    D_HEADS * D_DK, D_HEADS * D_DK, D_HEADS * D_DV, D_HEADS * D_DV,
)

AQ_W = 640
AK_W = 384
PB_W = 1152
PC_W = 512
PD_W = 1024
IN_W = AQ_W + AK_W + PB_W + PC_W + PD_W

VMEM_LIMIT = 56 * 1024 * 1024

TM_PROJ = 512
TT_GLA = 256
TT_LRU = 256
GLA_LEVELS = (32, 16, 8, 4, 2, 1)


def _rms(x, g):
    return x * lax.rsqrt(jnp.mean(x * x, axis=-1, keepdims=True) + EPS) * g


def _dot(a, b):
    return jnp.dot(a, b, preferred_element_type=F32)


def _dot_nt(a, b):
    return lax.dot_general(a, b, (((1,), (1,)), ((), ())), preferred_element_type=F32)


def _stack_heads(a, lane_head):
    return jnp.concatenate([jnp.where(lane_head == h, a, 0.0) for h in range(4)], axis=0)


def _params(*sem):
    return pltpu.CompilerParams(dimension_semantics=sem, vmem_limit_bytes=VMEM_LIMIT)


def _const_spec(shape):
    nd = len(shape)
    return pl.BlockSpec(shape, lambda *_: (0,) * nd)


def _inproj_kernel(x_ref, g_ref, w_ref, kvg_ref, aq_ref, ak_ref, ckt_ref, pb_ref, pc_ref, pd_ref):
    h = _rms(x_ref[...], g_ref[...]).astype(BF16)
    o = 0
    aq_ref[...] = _dot(h, w_ref[:, o:o + AQ_W])
    o += AQ_W
    ak = _dot(h, w_ref[:, o:o + AK_W])
    ckv = _rms(ak[:, :A_KV_LATENT], kvg_ref[...])
    ak_ref[:, :A_KV_LATENT] = ckv.astype(BF16)
    ak_ref[:, A_KV_LATENT:] = ak[:, A_KV_LATENT:].astype(BF16)
    ckt_ref[0] = ckv.T.astype(BF16)
    o += AK_W
    pb_ref[...] = _dot(h, w_ref[:, o:o + PB_W])
    o += PB_W
    pc_ref[...] = _dot(h, w_ref[:, o:o + PC_W])
    o += PC_W
    pd_ref[...] = _dot(h, w_ref[:, o:o + PD_W])


def _inproj(x2, g, w, kvg):
    n, d = x2.shape
    tm = min(TM_PROJ, n)
    def rows(wd, dt):
        return pl.BlockSpec((tm, wd), lambda i: (i, 0)), jax.ShapeDtypeStruct((n, wd), dt)

    ckt = (pl.BlockSpec((1, A_KV_LATENT, tm), lambda i: (i, 0, 0)),
           jax.ShapeDtypeStruct((n // tm, A_KV_LATENT, tm), BF16))
    outs = [rows(AQ_W, F32), rows(AK_W, BF16), ckt, rows(PB_W, F32), rows(PC_W, F32), rows(PD_W, F32)]
    return pl.pallas_call(
        _inproj_kernel,
        grid=(n // tm,),
        in_specs=[pl.BlockSpec((tm, d), lambda i: (i, 0)), _const_spec((1, d)),
                  _const_spec((d, IN_W)), _const_spec((1, A_KV_LATENT))],
        out_specs=[o[0] for o in outs],
        out_shape=[o[1] for o in outs],
        compiler_params=_params("parallel"),
        name="inproj",
    )(x2, g, w, kvg)


def _t5_bucket(dist):
    n = jnp.maximum(dist, 0)
    max_exact = N_BUCKETS // 2
    nf = jnp.maximum(n, max_exact).astype(F32)
    large = max_exact + (jnp.log(nf / max_exact) / math.log(MAX_DISTANCE / max_exact)
                         * (N_BUCKETS - max_exact)).astype(I32)
    large = jnp.minimum(large, N_BUCKETS - 1)
    return jnp.where(n < max_exact, n, large)


def _bias_tiles_kernel(rb_ref, out_ref):
    krow = lax.broadcasted_iota(I32, (Q_BLOCK, Q_BLOCK), 0)
    qcol = lax.broadcasted_iota(I32, (Q_BLOCK, Q_BLOCK), 1)
    for d in range(3):
        bucket = _t5_bucket(d * Q_BLOCK + qcol - krow)
        for h in range(A_HEADS):
            acc = jnp.zeros((Q_BLOCK, Q_BLOCK), F32)
            for k in range(N_BUCKETS):
                acc = jnp.where(bucket == k, rb_ref[k, h], acc)
            out_ref[d, :, h * Q_BLOCK:(h + 1) * Q_BLOCK] = acc


def _bias_tiles(rel_bias):
    return pl.pallas_call(
        _bias_tiles_kernel,
        in_specs=[pl.BlockSpec(memory_space=pltpu.SMEM)],
        out_specs=pl.BlockSpec(memory_space=pltpu.VMEM),
        out_shape=jax.ShapeDtypeStruct((3, Q_BLOCK, A_HEADS * Q_BLOCK), F32),
        name="bias_tiles",
    )(rel_bias)


def _dsa_kernel(aq_ref, ak_ref, ckt_ref, wuk_ref, wuvt_ref, bias_ref, tri_ref, out_ref,
                keys_ref, s_ref, *, topk, ksb):
    qb = Q_BLOCK
    ng = ksb // qb
    qi = pl.program_id(1)
    nsb = qi // ng + 1
    aq = aq_ref[...]
    lane_head = lax.broadcasted_iota(I32, (qb, HEAD_W), 1) // A_HEAD_DIM
    krow = lax.broadcasted_iota(I32, (qb, qb), 0)
    qpos = qi * qb + lax.broadcasted_iota(I32, (qb, qb), 1)

    def causal(sb, g):
        return sb * ksb + g * qb + krow <= qpos

    groups = [(g, slice(g * qb, (g + 1) * qb)) for g in range(ng)]
    heads = [(h, slice(h * qb, (h + 1) * qb)) for h in range(A_HEADS)]

    def fold_rows(a, op, rows):
        return op(a.reshape(a.shape[0] // rows, rows, a.shape[1]), axis=0)

    qx = _stack_heads(aq[:, 256:512], lane_head).astype(BF16)
    qlat = _dot(_stack_heads(aq[:, 0:256], lane_head).astype(BF16), wuk_ref[...])
    qlat = (qlat * (A_HEAD_DIM ** -0.5)).astype(BF16)
    iw_t = (aq[:, 512:640] * (IDX_HEADS ** -0.5 * IDX_DIM ** -0.5)).T
    iw_rows = [iw_t[h:h + 1, :] for h in range(IDX_HEADS)]

    def score_body(sb, _):
        off = pl.multiple_of(sb * ksb, ksb)
        ik = ak_ref[pl.ds(off, ksb), A_KV_LATENT:]
        rel = jnp.maximum(_dot_nt(ik, qx), 0.0)
        for g, rs in groups:
            sc = jnp.zeros((qb, qb), F32)
            for h, cs in heads:
                sc = sc + rel[rs, cs] * iw_rows[h]
            sc = jnp.where(causal(sb, g), sc, NEG_BIG)
            bits = pltpu.bitcast(sc, I32)
            keys_ref[pl.ds(off + g * qb, qb), :] = bits ^ ((bits >> 31) & 0x7FFFFFFF)
        return 0

    lax.fori_loop(0, nsb, score_body, 0)

    acc_rows = 32

    def count(pred_fn):
        def body(sb, acc):
            off = pl.multiple_of(sb * ksb, ksb)
            hit = jnp.where(pred_fn(keys_ref[pl.ds(off, ksb), :]), 1.0, 0.0)
            return acc + fold_rows(hit, jnp.sum, acc_rows)
        acc = lax.fori_loop(0, nsb, body, jnp.zeros((acc_rows, qb), F32))
        return jnp.sum(acc, axis=0, keepdims=True)

    def bit_body(i, theta):
        cand = theta + jnp.left_shift(jnp.int32(1), 31 - i)
        cnt = count(lambda key: key >= cand)
        return jnp.where(cnt >= topk, cand, theta)

    theta = lax.fori_loop(0, 32, bit_body, jnp.full((1, qb), INT_MIN, I32))
    need = topk - count(lambda key: key > theta)

    def logit_body(sb, carry):
        mx, tie_seen = carry
        off = pl.multiple_of(sb * ksb, ksb)
        ck = ak_ref[pl.ds(off, ksb), :A_KV_LATENT]
        st = _dot_nt(ck, qlat)
        key = keys_ref[pl.ds(off, ksb), :]
        eq = key == theta
        eqf = jnp.where(eq, 1.0, 0.0)
        rank = tie_seen + _dot(tri_ref[...], eqf.astype(BF16))
        tie_seen = rank[ksb - 1:ksb] + eqf[ksb - 1:ksb]
        sel = (key > theta) | (eq & (rank < need))
        mx = list(mx)
        for g, rs in groups:
            valid = sel[rs] & causal(sb, g)
            bias = bias_ref[jnp.clip(qi - (sb * ng + g), 0, 2)]
            for h, cs in heads:
                sh = jnp.where(valid, st[rs, cs] + bias[:, cs], NEG_BIG)
                s_ref[pl.ds(off + g * qb, qb), cs] = sh
                mx[h] = jnp.maximum(mx[h], fold_rows(sh, jnp.max, 8))
        return tuple(mx), tie_seen

    init = (tuple(jnp.full((8, qb), NEG_BIG, F32) for _ in range(A_HEADS)), jnp.zeros((1, qb), F32))
    mx, _ = lax.fori_loop(0, nsb, logit_body, init)
    m = jnp.concatenate([jnp.max(v, axis=0, keepdims=True) for v in mx], axis=1)

    ones = jnp.ones((A_KV_LATENT, ksb), BF16)

    def pv_body(sb, acc):
        off = pl.multiple_of(sb * ksb, ksb)
        p = jnp.exp(s_ref[pl.ds(off, ksb), :] - m).astype(BF16)
        return acc + _dot(jnp.concatenate([ckt_ref[sb], ones], axis=0), p)

    acc = lax.fori_loop(0, nsb, pv_body, jnp.zeros((2 * A_KV_LATENT, A_HEADS * qb), F32))
    o_t = (acc[:A_KV_LATENT] / acc[A_KV_LATENT:]).astype(BF16)
    y_t = jnp.zeros((HEAD_W, qb), F32)
    for h, cs in heads:
        y_t = y_t + _dot(wuvt_ref[h], o_t[:, cs])
    out_ref[...] = y_t.T


def _dsa(aq, ak, ckt, wuk, wuvt, bias_tiles):
    b, t, _ = aq.shape
    ksb = ckt.shape[2]
    assert t % ksb == 0 and ksb % Q_BLOCK == 0 and ckt.shape[0] * ksb == b * t
    nsb = t // ksb
    topk = min(MAX_TOPK, t // 4)
    idx = np.arange(ksb)
    tri = jnp.asarray(idx[None, :] < idx[:, None], BF16)
    return pl.pallas_call(
        functools.partial(_dsa_kernel, topk=topk, ksb=ksb),
        grid=(b, t // Q_BLOCK),
        in_specs=[pl.BlockSpec((None, Q_BLOCK, AQ_W), lambda i, j: (i, j, 0)),
                  pl.BlockSpec((None, t, AK_W), lambda i, j: (i, 0, 0)),
                  pl.BlockSpec((nsb, A_KV_LATENT, ksb), lambda i, j: (i, 0, 0)),
                  _const_spec(wuk.shape), _const_spec(wuvt.shape), _const_spec(bias_tiles.shape),
                  _const_spec(tri.shape)],
        out_specs=pl.BlockSpec((None, Q_BLOCK, HEAD_W), lambda i, j: (i, j, 0)),
        out_shape=jax.ShapeDtypeStruct((b, t, HEAD_W), F32),
        scratch_shapes=[pltpu.VMEM((t, Q_BLOCK), I32),
                        pltpu.VMEM((t, A_HEADS * Q_BLOCK), F32)],
        compiler_params=_params("parallel", "arbitrary"),
        name="dsa",
    )(aq, ak, ckt, wuk, wuvt, bias_tiles, tri)


def _gla_static_tables():
    c = CHUNK
    idx = np.arange(c)
    tri = (idx[None, :] <= idx[:, None]).astype(np.float32)
    blocks = [tri]
    for s in GLA_LEVELS:
        end_left = (idx // (2 * s)) * 2 * s + s - 1
        blocks.append(tri[end_left])
    blocks.append(np.ones((c, c), np.float32))
    lev = np.full((c, c), -1, np.int32)
    for i in range(c):
        for j in range(c):
            if i == j:
                lev[i, j] = len(GLA_LEVELS)
            elif i > j:
                s = 1 << int(math.floor(math.log2(i ^ j)))
                lev[i, j] = GLA_LEVELS.index(s)
    pool = np.kron(np.eye(4, dtype=np.float32), np.full((64, 64), 1.0 / 64, np.float32))
    return np.concatenate(blocks, axis=0), np.tile(lev, (1, 4)), pool


def _expm1(x):
    return jnp.tanh(0.5 * x) * (jnp.exp(x) + 1.0)


def _log_sigmoid(x):
    return -(jnp.maximum(-x, 0.0) + jnp.log1p(jnp.exp(-jnp.abs(x))))


def _gla_core(q, k, v, g, mall, lev, st):
    c = CHUNK
    lane_head = lax.broadcasted_iota(I32, (c, HEAD_W), 1) // 64
    rowc = lax.broadcasted_iota(I32, (c, HEAD_W), 0)
    bc = jnp.dot(mall, g, preferred_element_type=F32, precision=lax.Precision.HIGHEST)
    b = bc[0:c]
    blast = bc[(len(GLA_LEVELS) + 1) * c:]
    att = jnp.zeros((c, HEAD_W), F32)
    for li, s in enumerate(GLA_LEVELS):
        mid = bc[(li + 1) * c:(li + 2) * c]
        right = (rowc & s) != 0
        qt = jnp.where(right, q * jnp.exp(jnp.minimum(b - mid, 0.0)), 0.0)
        kt = jnp.where(right, 0.0, k * jnp.exp(jnp.minimum(mid - b, 0.0)))
        al = _dot_nt(qt.astype(BF16), _stack_heads(kt, lane_head).astype(BF16))
        att = jnp.where(lev == li, al, att)
    ad = _dot_nt(q.astype(BF16), _stack_heads(k, lane_head).astype(BF16))
    att = jnp.where(lev == len(GLA_LEVELS), ad, att)
    o = _dot(att.astype(BF16), _stack_heads(v, lane_head).astype(BF16))
    o = o + _dot_nt((q * jnp.exp(b)).astype(BF16), st.astype(BF16))
    ktail = k * jnp.exp(blast - b)
    upd = _dot(v.T.astype(BF16), ktail.astype(BF16))
    r2 = lax.broadcasted_iota(I32, (HEAD_W, HEAD_W), 0) // 64
    c2 = lax.broadcasted_iota(I32, (HEAD_W, HEAD_W), 1) // 64
    st = st * jnp.exp(blast[0:1, :]) + jnp.where(r2 == c2, upd, 0.0)
    return o, st


def _gla_finish(o, r, ng, pool):
    ms = jnp.dot(o * o, pool, preferred_element_type=F32, precision=lax.Precision.HIGHEST)
    return o * lax.rsqrt(ms + EPS) * ng * jax.nn.silu(r)


def _gla_b_kernel(p_ref, wgk_ref, bgk_ref, ng_ref, mall_ref, lev_ref, pool_ref, out_ref, st_ref):
    @pl.when(pl.program_id(1) == 0)
    def _():
        st_ref[...] = jnp.zeros_like(st_ref)

    tile = p_ref[...]
    q = tile[:, 0:256] * (B_DK ** -0.5)
    k = tile[:, 256:512]
    v = tile[:, 512:768]
    r = tile[:, 768:1024]
    z = _dot(tile[:, 1024:1152].astype(BF16), wgk_ref[...]) + bgk_ref[...]
    g = _log_sigmoid(z) / B_GATE_TAU
    mall, lev, st = mall_ref[...], lev_ref[...], st_ref[...]
    outs = []
    for c in range(tile.shape[0] // CHUNK):
        sl = slice(c * CHUNK, (c + 1) * CHUNK)
        o, st = _gla_core(q[sl], k[sl], v[sl], g[sl], mall, lev, st)
        outs.append(o)
    st_ref[...] = st
    out_ref[...] = _gla_finish(jnp.concatenate(outs, axis=0), r, ng_ref[...], pool_ref[...])


def _gla_d_kernel(p_ref, lbp_ref, ng_ref, mall_ref, lev_ref, pool_ref, out_ref, st_ref, *, layer):
    @pl.when(pl.program_id(1) == 0)
    def _():
        st_ref[...] = jnp.zeros_like(st_ref)

    lbp = lbp_ref[...]
    e = jnp.exp(lbp - jnp.max(lbp, axis=0, keepdims=True))
    soft = e / jnp.sum(e, axis=0, keepdims=True)
    cum = soft[0:1]
    for i in range(1, layer + 1):
        cum = cum + soft[i:i + 1]
    lb = cum - soft[0:1]

    tile = p_ref[...]
    q = jax.nn.silu(tile[:, 0:256])
    v = tile[:, 512:768]
    r = tile[:, 768:1024]
    t1 = jnp.log(jnp.maximum(lb, LB_FLOOR))
    t2 = jnp.log1p(-lb) + _log_sigmoid(tile[:, 256:512])
    g = jnp.maximum(t1, t2) + jnp.log1p(jnp.exp(-jnp.abs(t1 - t2)))
    k = -_expm1(g)
    mall, lev, st = mall_ref[...], lev_ref[...], st_ref[...]
    outs = []
    for c in range(tile.shape[0] // CHUNK):
        sl = slice(c * CHUNK, (c + 1) * CHUNK)
        o, st = _gla_core(q[sl], k[sl], v[sl], g[sl], mall, lev, st)
        outs.append(o)
    st_ref[...] = st
    out_ref[...] = _gla_finish(jnp.concatenate(outs, axis=0), r, ng_ref[...], pool_ref[...])


def _gla_call(kernel, p, extra, name):
    b, t, w = p.shape
    tt = min(TT_GLA, t)
    mall, lev, pool = _gla_static_tables()
    consts = list(extra) + [jnp.asarray(mall), jnp.asarray(lev), jnp.asarray(pool)]
    return pl.pallas_call(
        kernel,
        grid=(b, t // tt),
        in_specs=[pl.BlockSpec((None, tt, w), lambda i, j: (i, j, 0))]
        + [_const_spec(c.shape) for c in consts],
        out_specs=pl.BlockSpec((None, tt, HEAD_W), lambda i, j: (i, j, 0)),
        out_shape=jax.ShapeDtypeStruct((b, t, HEAD_W), F32),
        scratch_shapes=[pltpu.VMEM((HEAD_W, HEAD_W), F32)],
        compiler_params=_params("parallel", "arbitrary"),
        name=name,
    )(p, *consts)


def _lru_kernel(p_ref, cw_ref, cb_ref, wa_ref, ba_ref, wx_ref, bx_ref, lam_ref, out_ref,
                xpad_ref, h_ref):
    tt = p_ref.shape[0]
    pad = 8

    @pl.when(pl.program_id(1) == 0)
    def _():
        xpad_ref[0:pad, :] = jnp.zeros((pad, C_WIDTH), F32)
        h_ref[...] = jnp.zeros_like(h_ref)

    x = p_ref[:, 0:C_WIDTH]
    xpad_ref[pad:pad + tt, :] = x
    base = pad - (C_CONV - 1)
    xc = xpad_ref[base:base + tt, :] * cw_ref[0:1, :]
    for j in range(1, C_CONV):
        xc = xc + xpad_ref[base + j:base + j + tt, :] * cw_ref[j:j + 1, :]
    xc = xc + cb_ref[...]
    xpad_ref[0:pad, :] = x[tt - pad:tt, :]

    xb = xc.astype(BF16)
    r = jax.nn.sigmoid(_dot(xb, wa_ref[...]) + ba_ref[...])
    i = jax.nn.sigmoid(_dot(xb, wx_ref[...]) + bx_ref[...])
    lam = lam_ref[...]
    softplus_neg_lam = jnp.maximum(-lam, 0.0) + jnp.log1p(jnp.exp(-jnp.abs(lam)))
    log_a = -C_EXP * r * softplus_neg_lam
    a = jnp.exp(log_a)
    u = jnp.sqrt(jnp.maximum(-_expm1(2.0 * log_a), 0.0)) * (i * xc)

    rows = lax.broadcasted_iota(I32, (tt, C_WIDTH), 0)
    s = 1
    while s < tt:
        keep = rows >= s
        u = jnp.where(keep, a * pltpu.roll(u, s, 0) + u, u)
        a = jnp.where(keep, a * pltpu.roll(a, s, 0), a)
        s *= 2
    h = u + a * h_ref[0:1, :]
    h_ref[...] = jnp.broadcast_to(h[tt - 1:tt, :], h_ref.shape)
    out_ref[...] = h * jax.nn.gelu(p_ref[:, C_WIDTH:2 * C_WIDTH])


def _lru(pc, cw, cb, wa, ba, wx, bx, lam):
    b, t, w = pc.shape
    tt = min(TT_LRU, t)
    consts = [cw, cb, wa, ba, wx, bx, lam]
    return pl.pallas_call(
        _lru_kernel,
        grid=(b, t // tt),
        in_specs=[pl.BlockSpec((None, tt, w), lambda i, j: (i, j, 0))]
        + [_const_spec(c.shape) for c in consts],
        out_specs=pl.BlockSpec((None, tt, C_WIDTH), lambda i, j: (i, j, 0)),
        out_shape=jax.ShapeDtypeStruct((b, t, C_WIDTH), F32),
        scratch_shapes=[pltpu.VMEM((tt + 8, C_WIDTH), F32), pltpu.VMEM((8, C_WIDTH), F32)],
        compiler_params=_params("parallel", "arbitrary"),
        name="rglru",
    )(pc, *consts)


def _merge_kernel(x_ref, ya_ref, yb_ref, yc_ref, yd_ref, g_ref, wg_ref, bg_ref, wbr_ref, wo_ref,
                  out_ref):
    x = x_ref[...]
    d = x.shape[1]
    h = _rms(x, g_ref[...]).astype(BF16)
    merged = jnp.zeros(x.shape, F32)
    for n, y_ref in enumerate((ya_ref, yb_ref, yc_ref, yd_ref)):
        gate = jax.nn.sigmoid(_dot(h, wg_ref[:, n * d:(n + 1) * d]) + bg_ref[:, n * d:(n + 1) * d])
        merged = merged + gate * _dot(y_ref[...].astype(BF16), wbr_ref[n])
    out_ref[...] = x + _dot(merged.astype(BF16), wo_ref[...])


def _merge(x2, ys, g, wg, bg, wbr, wo):
    n, d = x2.shape
    tm = min(TM_PROJ, n)
    consts = [g, wg, bg, wbr, wo]
    return pl.pallas_call(
        _merge_kernel,
        grid=(n // tm,),
        in_specs=[pl.BlockSpec((tm, d), lambda i: (i, 0))]
        + [pl.BlockSpec((tm, HEAD_W), lambda i: (i, 0)) for _ in ys]
        + [_const_spec(c.shape) for c in consts],
        out_specs=pl.BlockSpec((tm, d), lambda i: (i, 0)),
        out_shape=jax.ShapeDtypeStruct((n, d), F32),
        compiler_params=_params("parallel"),
        name="merge",
    )(x2, *ys, *consts)


def _ffn_kernel(x_ref, g_ref, w1_ref, w2_ref, gf_ref, out_ref, *, final):
    x = x_ref[...]
    d = x.shape[1]
    h = _rms(x, g_ref[...]).astype(BF16)
    acc = x
    for c in range(w1_ref.shape[1] // d):
        hid = jnp.square(jnp.maximum(_dot(h, w1_ref[:, c * d:(c + 1) * d]), 0.0))
        acc = acc + _dot(hid.astype(BF16), w2_ref[c * d:(c + 1) * d, :])
    if final:
        acc = _rms(acc, gf_ref[...])
    out_ref[...] = acc


def _ffn(x2, g, w1, w2, gf, final):
    n, d = x2.shape
    tm = min(TM_PROJ, n)
    consts = [g, w1, w2, gf]
    return pl.pallas_call(
        functools.partial(_ffn_kernel, final=final),
        grid=(n // tm,),
        in_specs=[pl.BlockSpec((tm, d), lambda i: (i, 0))] + [_const_spec(c.shape) for c in consts],
        out_specs=pl.BlockSpec((tm, d), lambda i: (i, 0)),
        out_shape=jax.ShapeDtypeStruct((n, d), F32),
        compiler_params=_params("parallel"),
        name="ffn",
    )(x2, *consts)


def _permute_w_in(w):
    offs = np.concatenate([[0], np.cumsum(IN_SPLITS)])
    col = lambda i: w[:, offs[i]:offs[i + 1]]
    zeros = lambda n: jnp.zeros((w.shape[0], n), w.dtype)
    (a_q, a_ckv, a_iq, a_ik, a_iw, b_q, b_k, b_v, b_lr, b_r, c_x, c_y, d_q, d_f, d_i, d_g) = (
        col(i) for i in range(len(IN_SPLITS)))
    parts = [a_q, a_iq, a_iw, zeros(LANES - IDX_HEADS),
             a_ckv, a_ik, a_ik, a_ik, a_ik,
             b_q, b_k, b_v, b_r, b_lr, zeros(LANES - B_GATE_RANK),
             c_x, c_y, d_q, d_f, d_i, d_g]
    return jnp.concatenate(parts, axis=1).astype(BF16)


def _block_diag(w):
    n, bi, bj = w.shape
    eye = jnp.eye(n, dtype=w.dtype)
    return (w[:, :, None, :] * eye[:, None, :, None]).reshape(n * bi, n * bj)


def _wide_heads(w):
    h, c, d = w.shape
    eye = jnp.eye(h, dtype=w.dtype)
    return (w[:, :, None, :] * eye[:, None, :, None]).reshape(h, c, h * d)


def kernel(x, norm1_g, w_in, w_gate, b_gate, kv_norm_g, w_uk, w_uv, rel_bias, w_gk2, b_gk, gla_norm_g, conv_w, conv_b, w_rg_a, b_rg_a, w_rg_x, b_rg_x, lru_lambda, lb_param, hgrn_norm_g, w_br_a, w_br_b, w_br_c, w_br_d, w_out, norm2_g, w_ff1, w_ff2, final_norm_g):
    bsz, t, d = x.shape
    depth = w_in.shape[0]
    n = bsz * t
    row = lambda v: v.reshape(1, -1)
    bias_tiles = _bias_tiles(rel_bias)
    x2 = x.reshape(n, d)
    for l in range(depth):
        aq, ak, ckt, pb, pc, pd = _inproj(x2, row(norm1_g[l]), _permute_w_in(w_in[l]), row(kv_norm_g[l]))
        to3 = lambda a: a.reshape(bsz, t, a.shape[-1])

        y_a = _dsa(to3(aq), to3(ak), ckt, w_uk[l].reshape(A_HEADS * A_HEAD_DIM, A_KV_LATENT).astype(BF16),
                   jnp.swapaxes(_wide_heads(w_uv[l]), 1, 2).astype(BF16), bias_tiles)

        wgk = jnp.concatenate([w_gk2[l], jnp.zeros((LANES - B_GATE_RANK, HEAD_W), F32)], axis=0)
        y_b = _gla_call(_gla_b_kernel, to3(pb),
                        [wgk.astype(BF16), row(b_gk[l]), row(jnp.tile(gla_norm_g[l], B_HEADS))], "gla")

        y_c = _lru(to3(pc), conv_w[l], row(conv_b[l]), _block_diag(w_rg_a[l]).astype(BF16),
                   row(b_rg_a[l]), _block_diag(w_rg_x[l]).astype(BF16), row(b_rg_x[l]),
                   row(lru_lambda[l]))

        y_d = _gla_call(functools.partial(_gla_d_kernel, layer=l), to3(pd),
                        [lb_param, row(jnp.tile(hgrn_norm_g[l], D_HEADS))], "hgrn2")

        ys = [y.reshape(n, HEAD_W) for y in (y_a, y_b, y_c, y_d)]
        wbr = jnp.stack([w_br_a[l], w_br_b[l], w_br_c[l], w_br_d[l]]).astype(BF16)
        x2 = _merge(x2, ys, row(norm1_g[l]), w_gate[l].astype(BF16), row(b_gate[l]), wbr,
                    w_out[l].astype(BF16))
        x2 = _ffn(x2, row(norm2_g[l]), w_ff1[l].astype(BF16), w_ff2[l].astype(BF16),
                  row(final_norm_g), final=(l == depth - 1))
    return x2.reshape(bsz, t, d)
```

```python
import functools
import math

import numpy as np
import jax
import jax.numpy as jnp
from jax import lax
from jax.experimental import pallas as pl
from jax.experimental.pallas import tpu as pltpu

F32 = jnp.float32
BF16 = jnp.bfloat16
I32 = jnp.int32
I16 = jnp.int16

EPS = 1e-6
NEG_BIG = -1e30
UNSELECTED = -3e38
LB_FLOOR = 1e-20
I16_MIN = -(2 ** 15)
LOG2E = 1.4426950408889634

A_HEADS = 4
A_HEAD_DIM = 64
A_KV_LATENT = 128
IDX_HEADS = 4
IDX_DIM = 64
MAX_TOPK = 256
Q_BLOCK = 128
N_BUCKETS = 32
MAX_DISTANCE = 128
B_HEADS = 4
B_DK = 64
B_DV = 64
B_GATE_RANK = 16
B_GATE_TAU = 16.0
C_WIDTH = 256
C_BLOCKS = 4
C_CONV = 4
C_EXP = 8.0
D_HEADS = 4
D_DK = 64
D_DV = 64
CHUNK = 64
N_BRANCH = 4

HEAD_W = 256
LANES = 128

IN_SPLITS = (
    A_HEADS * A_HEAD_DIM, A_KV_LATENT, IDX_HEADS * IDX_DIM, IDX_DIM, IDX_HEADS,
    B_HEADS * B_DK, B_HEADS * B_DK, B_HEADS * B_DV, B_GATE_RANK, B_HEADS * B_DV,
    C_WIDTH, C_WIDTH,
    D_HEADS * D_DK, D_HEADS * D_DK, D_HEADS * D_DV, D_HEADS * D_DV,
)

AQ_W = 640
AK_W = 384
PB_W = 1152
PC_W = 512
PD_W = 1024
IN_W = AQ_W + AK_W + PB_W + PC_W + PD_W

VMEM_LIMIT = 56 * 1024 * 1024

TM_PROJ = 512
TT_GLA = 256
TT_LRU = 256
GLA_LEVELS = (32, 16, 8, 4, 2, 1)


def _rms(x, g):
    return x * lax.rsqrt(jnp.mean(x * x, axis=-1, keepdims=True) + EPS) * g


def _dot(a, b):
    return jnp.dot(a, b, preferred_element_type=F32)


def _dot_nt(a, b):
    return lax.dot_general(a, b, (((1,), (1,)), ((), ())), preferred_element_type=F32)


def _split3(x):
    hi = x.astype(BF16)
    rest = x - hi.astype(F32)
    mid = rest.astype(BF16)
    lo = (rest - mid.astype(F32)).astype(BF16)
    return lo, mid, hi


def _stack_heads(a, lane_head):
    return jnp.concatenate([jnp.where(lane_head == h, a, 0.0) for h in range(4)], axis=0)


def _params(*sem):
    return pltpu.CompilerParams(dimension_semantics=sem, vmem_limit_bytes=VMEM_LIMIT)


def _const_spec(shape):
    nd = len(shape)
    return pl.BlockSpec(shape, lambda *_: (0,) * nd)


def _inproj_kernel(x_ref, g_ref, w_ref, kvg_ref, aq_ref, ak_ref, ckt_ref, pb_ref, pc_ref, pd_ref):
    h = _rms(x_ref[...], g_ref[...]).astype(BF16)
    o = 0
    aq_ref[...] = _dot(h, w_ref[:, o:o + AQ_W])
    o += AQ_W
    ak = _dot(h, w_ref[:, o:o + AK_W])
    ckv = _rms(ak[:, :A_KV_LATENT], kvg_ref[...])
    ak_ref[:, :A_KV_LATENT] = ckv.astype(BF16)
    ak_ref[:, A_KV_LATENT:] = ak[:, A_KV_LATENT:].astype(BF16)
    ckt_ref[0] = ckv.T.astype(BF16)
    o += AK_W
    pb_ref[...] = _dot(h, w_ref[:, o:o + PB_W])
    o += PB_W
    pc_ref[...] = _dot(h, w_ref[:, o:o + PC_W])
    o += PC_W
    pd_ref[...] = _dot(h, w_ref[:, o:o + PD_W])


def _inproj(x2, g, w, kvg):
    n, d = x2.shape
    tm = min(TM_PROJ, n)
    def rows(wd, dt):
        return pl.BlockSpec((tm, wd), lambda i: (i, 0)), jax.ShapeDtypeStruct((n, wd), dt)

    ckt = (pl.BlockSpec((1, A_KV_LATENT, tm), lambda i: (i, 0, 0)),
           jax.ShapeDtypeStruct((n // tm, A_KV_LATENT, tm), BF16))
    outs = [rows(AQ_W, F32), rows(AK_W, BF16), ckt, rows(PB_W, F32), rows(PC_W, F32), rows(PD_W, F32)]
    return pl.pallas_call(
        _inproj_kernel,
        grid=(n // tm,),
        in_specs=[pl.BlockSpec((tm, d), lambda i: (i, 0)), _const_spec((1, d)),
                  _const_spec((d, IN_W)), _const_spec((1, A_KV_LATENT))],
        out_specs=[o[0] for o in outs],
        out_shape=[o[1] for o in outs],
        compiler_params=_params("parallel"),
        name="inproj",
    )(x2, g, w, kvg)


def _t5_bucket(dist):
    n = jnp.maximum(dist, 0)
    max_exact = N_BUCKETS // 2
    nf = jnp.maximum(n, max_exact).astype(F32)
    large = max_exact + (jnp.log(nf / max_exact) / math.log(MAX_DISTANCE / max_exact)
                         * (N_BUCKETS - max_exact)).astype(I32)
    large = jnp.minimum(large, N_BUCKETS - 1)
    return jnp.where(n < max_exact, n, large)


def _bias_tiles_kernel(rb_ref, out_ref):
    krow = lax.broadcasted_iota(I32, (Q_BLOCK, Q_BLOCK), 0)
    qcol = lax.broadcasted_iota(I32, (Q_BLOCK, Q_BLOCK), 1)
    for d in range(3):
        bucket = _t5_bucket(d * Q_BLOCK + qcol - krow)
        for h in range(A_HEADS):
            acc = jnp.zeros((Q_BLOCK, Q_BLOCK), F32)
            for k in range(N_BUCKETS):
                acc = jnp.where(bucket == k, rb_ref[k, h], acc)
            out_ref[d, :, h * Q_BLOCK:(h + 1) * Q_BLOCK] = acc


def _bias_tiles(rel_bias):
    return pl.pallas_call(
        _bias_tiles_kernel,
        in_specs=[pl.BlockSpec(memory_space=pltpu.SMEM)],
        out_specs=pl.BlockSpec(memory_space=pltpu.VMEM),
        out_shape=jax.ShapeDtypeStruct((3, Q_BLOCK, A_HEADS * Q_BLOCK), F32),
        name="bias_tiles",
    )(rel_bias)


def _dsa_kernel(aq_ref, ak_ref, ckt_ref, wuk_ref, wuvt_ref, bias_ref, tri_ref, out_ref,
                keys_ref, khi_ref, klo_ref, s_ref, *, topk, ksb):
    qb = Q_BLOCK
    ng = ksb // qb
    qi = pl.program_id(1)
    nsb = qi // ng + 1
    aq = aq_ref[...]
    lane_head = lax.broadcasted_iota(I32, (qb, HEAD_W), 1) // A_HEAD_DIM
    krow = lax.broadcasted_iota(I32, (qb, qb), 0)
    qpos = qi * qb + lax.broadcasted_iota(I32, (qb, qb), 1)

    def causal(sb, g):
        return sb * ksb + g * qb + krow <= qpos

    groups = [(g, slice(g * qb, (g + 1) * qb)) for g in range(ng)]
    heads = [(h, slice(h * qb, (h + 1) * qb)) for h in range(A_HEADS)]

    def fold_rows(a, op, rows):
        return op(a.reshape(a.shape[0] // rows, rows, a.shape[1]), axis=0)

    qx = _stack_heads(aq[:, 256:512], lane_head).astype(BF16)
    qlat = _dot(_stack_heads(aq[:, 0:256], lane_head).astype(BF16), wuk_ref[...])
    qlat = (qlat * (A_HEAD_DIM ** -0.5)).astype(BF16)
    iw_t = (aq[:, 512:640] * (IDX_HEADS ** -0.5 * IDX_DIM ** -0.5)).T
    iw_rows = [iw_t[h:h + 1, :] for h in range(IDX_HEADS)]

    def score_body(sb, _):
        off = pl.multiple_of(sb * ksb, ksb)
        ik = ak_ref[pl.ds(off, ksb), A_KV_LATENT:]
        ck = ak_ref[pl.ds(off, ksb), :A_KV_LATENT]
        rel = jnp.maximum(_dot_nt(ik, qx), 0.0)
        st = _dot_nt(ck, qlat)
        for g, rs in groups:
            sc = jnp.zeros((qb, qb), F32)
            for h, cs in heads:
                sc = sc + rel[rs, cs] * iw_rows[h]
            sc = jnp.where(causal(sb, g), sc, NEG_BIG)
            bits = pltpu.bitcast(sc, I32)
            key = bits ^ ((bits >> 31) & 0x7FFFFFFF)
            rows = pl.ds(off + g * qb, qb)
            keys_ref[rows, :] = key
            khi_ref[rows, :] = (key >> 16).astype(I16)
            klo_ref[rows, :] = ((key & 0xFFFF) + I16_MIN).astype(I16)
            bias = bias_ref[jnp.clip(qi - (sb * ng + g), 0, 2)]
            s_ref[rows, :] = (st[rs] + bias) * LOG2E
        return 0

    lax.fori_loop(0, nsb, score_body, 0)

    acc_rows = 32

    def count16(ref, pred_fn):
        def body(sb, acc):
            off = pl.multiple_of(sb * ksb, ksb)
            hit = jnp.where(pred_fn(ref[pl.ds(off, ksb), :]), jnp.int16(1), jnp.int16(0))
            parts = [hit[r:r + acc_rows] for r in range(0, ksb, acc_rows)]
            while len(parts) > 1:
                parts = [a + b for a, b in zip(parts[::2], parts[1::2])]
            return acc + parts[0]
        acc = lax.fori_loop(0, nsb, body, jnp.zeros((acc_rows, qb), I16))
        return jnp.sum(acc.astype(I32), axis=0, keepdims=True)

    def kth_largest16(ref, k_row):
        def bit_body(i, carry):
            theta, cnt_ge = carry
            cand = theta + jnp.left_shift(jnp.int32(1), 15 - i)
            cand16 = cand.astype(I16)
            cnt = count16(ref, lambda key: key >= cand16)
            ok = cnt >= k_row
            return jnp.where(ok, cand, theta), jnp.where(ok, cnt, cnt_ge)
        init = (jnp.full((1, qb), I16_MIN, I32), jnp.full((1, qb), nsb * ksb, I32))
        return lax.fori_loop(0, 16, bit_body, init)

    theta_hi, _ = kth_largest16(khi_ref, topk)
    th16 = theta_hi.astype(I16)
    k_lo = topk - count16(khi_ref, lambda key: key > th16)

    def bucket_body(sb, _):
        rows = pl.ds(pl.multiple_of(sb * ksb, ksb), ksb)
        klo_ref[rows, :] = jnp.where(khi_ref[rows, :] == th16, klo_ref[rows, :], jnp.int16(I16_MIN))
        return 0

    lax.fori_loop(0, nsb, bucket_body, 0)
    theta_lo, cnt_ge = kth_largest16(klo_ref, k_lo)
    tl16 = theta_lo.astype(I16)
    theta = theta_hi * 65536 + (theta_lo - I16_MIN)
    need = (k_lo - count16(klo_ref, lambda key: key > tl16)).astype(F32)
    tie_break = jnp.max(cnt_ge - k_lo) > 0

    ones = jnp.ones((A_KV_LATENT, ksb), BF16)

    def attend_body(ranked, sb, carry):
        m, acc, tie_seen = carry
        off = pl.multiple_of(sb * ksb, ksb)
        key = keys_ref[pl.ds(off, ksb), :]
        if ranked:
            eq = key == theta
            eqf = jnp.where(eq, 1.0, 0.0)
            rank = tie_seen + _dot(tri_ref[...], eqf.astype(BF16))
            tie_seen = rank[ksb - 1:ksb] + eqf[ksb - 1:ksb]
            sel = (key > theta) | (eq & (rank < need))
        else:
            sel = key >= theta
        blocks, mx = [], []
        for g, rs in groups:
            valid = sel[rs] & causal(sb, g)
            raw = s_ref[pl.ds(off + g * qb, qb), :]
            blocks.append(jnp.concatenate(
                [jnp.where(valid, raw[:, cs], UNSELECTED) for _, cs in heads], axis=1))
            mx.append(fold_rows(blocks[-1], jnp.max, 8))
        mx = jnp.max(functools.reduce(jnp.maximum, mx), axis=0, keepdims=True)
        m_new = jnp.maximum(m, mx)
        p = jnp.exp2(jnp.concatenate(blocks, axis=0) - m_new).astype(BF16)
        acc = acc * jnp.exp2(m - m_new) + _dot(jnp.concatenate([ckt_ref[sb], ones], axis=0), p)
        return m_new, acc, tie_seen

    def attend(ranked):
        init = (jnp.full((1, A_HEADS * qb), NEG_BIG, F32),
                jnp.zeros((2 * A_KV_LATENT, A_HEADS * qb), F32), jnp.zeros((1, qb), F32))
        return lax.fori_loop(0, nsb, functools.partial(attend_body, ranked), init)[1]

    acc = lax.cond(tie_break, lambda: attend(True), lambda: attend(False))
    o_t = (acc[:A_KV_LATENT] / acc[A_KV_LATENT:]).astype(BF16)
    y_t = jnp.zeros((HEAD_W, qb), F32)
    for h, cs in heads:
        y_t = y_t + _dot(wuvt_ref[h], o_t[:, cs])
    out_ref[...] = y_t.T


def _dsa(aq, ak, ckt, wuk, wuvt, bias_tiles):
    b, t, _ = aq.shape
    ksb = ckt.shape[2]
    assert t % ksb == 0 and ksb % Q_BLOCK == 0 and ckt.shape[0] * ksb == b * t
    nsb = t // ksb
    topk = min(MAX_TOPK, t // 4)
    idx = np.arange(ksb)
    tri = jnp.asarray(idx[None, :] < idx[:, None], BF16)
    return pl.pallas_call(
        functools.partial(_dsa_kernel, topk=topk, ksb=ksb),
        grid=(b, t // Q_BLOCK),
        in_specs=[pl.BlockSpec((None, Q_BLOCK, AQ_W), lambda i, j: (i, j, 0)),
                  pl.BlockSpec((None, t, AK_W), lambda i, j: (i, 0, 0)),
                  pl.BlockSpec((nsb, A_KV_LATENT, ksb), lambda i, j: (i, 0, 0)),
                  _const_spec(wuk.shape), _const_spec(wuvt.shape), _const_spec(bias_tiles.shape),
                  _const_spec(tri.shape)],
        out_specs=pl.BlockSpec((None, Q_BLOCK, HEAD_W), lambda i, j: (i, j, 0)),
        out_shape=jax.ShapeDtypeStruct((b, t, HEAD_W), F32),
        scratch_shapes=[pltpu.VMEM((t, Q_BLOCK), I32), pltpu.VMEM((t, Q_BLOCK), I16),
                        pltpu.VMEM((t, Q_BLOCK), I16), pltpu.VMEM((t, A_HEADS * Q_BLOCK), F32)],
        compiler_params=_params("parallel", "arbitrary"),
        name="dsa",
    )(aq, ak, ckt, wuk, wuvt, bias_tiles, tri)


def _gla_static_tables():
    c = CHUNK
    idx = np.arange(c)
    tri = (idx[None, :] <= idx[:, None]).astype(np.float32)
    blocks = [tri]
    for s in GLA_LEVELS:
        end_left = (idx // (2 * s)) * 2 * s + s - 1
        blocks.append(tri[end_left])
    blocks.append(np.ones((c, c), np.float32))
    lev = np.full((c, c), -1, np.int32)
    for i in range(c):
        for j in range(c):
            if i == j:
                lev[i, j] = len(GLA_LEVELS)
            elif i > j:
                s = 1 << int(math.floor(math.log2(i ^ j)))
                lev[i, j] = GLA_LEVELS.index(s)
    pool = np.kron(np.eye(4, dtype=np.float32), np.full((64, 64), 1.0 / 64, np.float32))
    return np.concatenate(blocks, axis=0), np.tile(lev, (1, 4)), pool


def _expm1(x):
    return jnp.tanh(0.5 * x) * (jnp.exp(x) + 1.0)


def _log_sigmoid(x):
    return -(jnp.maximum(-x, 0.0) + jnp.log1p(jnp.exp(-jnp.abs(x))))


def _gla_core(q, k, v, g, mall, lev, st):
    c = CHUNK
    lane_head = lax.broadcasted_iota(I32, (c, HEAD_W), 1) // 64
    rowc = lax.broadcasted_iota(I32, (c, HEAD_W), 0)
    bc = sum(_dot(mall, part) for part in _split3(g))
    b = bc[0:c]
    blast = bc[(len(GLA_LEVELS) + 1) * c:]
    att = jnp.zeros((c, HEAD_W), F32)
    for li, s in enumerate(GLA_LEVELS):
        mid = bc[(li + 1) * c:(li + 2) * c]
        right = (rowc & s) != 0
        qt = jnp.where(right, q * jnp.exp(jnp.minimum(b - mid, 0.0)), 0.0)
        kt = jnp.where(right, 0.0, k * jnp.exp(jnp.minimum(mid - b, 0.0)))
        al = _dot_nt(qt.astype(BF16), _stack_heads(kt, lane_head).astype(BF16))
        att = jnp.where(lev == li, al, att)
    ad = _dot_nt(q.astype(BF16), _stack_heads(k, lane_head).astype(BF16))
    att = jnp.where(lev == len(GLA_LEVELS), ad, att)
    o = _dot(att.astype(BF16), _stack_heads(v, lane_head).astype(BF16))
    o = o + _dot_nt((q * jnp.exp(b)).astype(BF16), st.astype(BF16))
    ktail = k * jnp.exp(blast - b)
    upd = _dot(v.T.astype(BF16), ktail.astype(BF16))
    r2 = lax.broadcasted_iota(I32, (HEAD_W, HEAD_W), 0) // 64
    c2 = lax.broadcasted_iota(I32, (HEAD_W, HEAD_W), 1) // 64
    st = st * jnp.exp(blast[0:1, :]) + jnp.where(r2 == c2, upd, 0.0)
    return o, st


def _gla_finish(o, r, ng, pool):
    ms = sum(_dot(part, pool) for part in _split3(o * o))
    return o * lax.rsqrt(ms + EPS) * ng * jax.nn.silu(r)


def _gla_b_kernel(p_ref, wgk_ref, bgk_ref, ng_ref, mall_ref, lev_ref, pool_ref, out_ref, st_ref):
    @pl.when(pl.program_id(1) == 0)
    def _():
        st_ref[...] = jnp.zeros_like(st_ref)

    tile = p_ref[...]
    q = tile[:, 0:256] * (B_DK ** -0.5)
    k = tile[:, 256:512]
    v = tile[:, 512:768]
    r = tile[:, 768:1024]
    z = _dot(tile[:, 1024:1152].astype(BF16), wgk_ref[...]) + bgk_ref[...]
    g = _log_sigmoid(z) / B_GATE_TAU
    mall, lev, st = mall_ref[...], lev_ref[...], st_ref[...]
    outs = []
    for c in range(tile.shape[0] // CHUNK):
        sl = slice(c * CHUNK, (c + 1) * CHUNK)
        o, st = _gla_core(q[sl], k[sl], v[sl], g[sl], mall, lev, st)
        outs.append(o)
    st_ref[...] = st
    out_ref[...] = _gla_finish(jnp.concatenate(outs, axis=0), r, ng_ref[...], pool_ref[...])


def _gla_d_kernel(p_ref, lbp_ref, ng_ref, mall_ref, lev_ref, pool_ref, out_ref, st_ref, *, layer):
    @pl.when(pl.program_id(1) == 0)
    def _():
        st_ref[...] = jnp.zeros_like(st_ref)

    lbp = lbp_ref[...]
    e = jnp.exp(lbp - jnp.max(lbp, axis=0, keepdims=True))
    soft = e / jnp.sum(e, axis=0, keepdims=True)
    cum = soft[0:1]
    for i in range(1, layer + 1):
        cum = cum + soft[i:i + 1]
    lb = cum - soft[0:1]

    tile = p_ref[...]
    q = jax.nn.silu(tile[:, 0:256])
    v = tile[:, 512:768]
    r = tile[:, 768:1024]
    t1 = jnp.log(jnp.maximum(lb, LB_FLOOR))
    t2 = jnp.log1p(-lb) + _log_sigmoid(tile[:, 256:512])
    g = jnp.maximum(t1, t2) + jnp.log1p(jnp.exp(-jnp.abs(t1 - t2)))
    k = -_expm1(g)
    mall, lev, st = mall_ref[...], lev_ref[...], st_ref[...]
    outs = []
    for c in range(tile.shape[0] // CHUNK):
        sl = slice(c * CHUNK, (c + 1) * CHUNK)
        o, st = _gla_core(q[sl], k[sl], v[sl], g[sl], mall, lev, st)
        outs.append(o)
    st_ref[...] = st
    out_ref[...] = _gla_finish(jnp.concatenate(outs, axis=0), r, ng_ref[...], pool_ref[...])


def _gla_call(kernel, p, extra, name):
    b, t, w = p.shape
    tt = min(TT_GLA, t)
    mall, lev, pool = _gla_static_tables()
    consts = list(extra) + [jnp.asarray(mall, BF16), jnp.asarray(lev), jnp.asarray(pool, BF16)]
    return pl.pallas_call(
        kernel,
        grid=(b, t // tt),
        in_specs=[pl.BlockSpec((None, tt, w), lambda i, j: (i, j, 0))]
        + [_const_spec(c.shape) for c in consts],
        out_specs=pl.BlockSpec((None, tt, HEAD_W), lambda i, j: (i, j, 0)),
        out_shape=jax.ShapeDtypeStruct((b, t, HEAD_W), F32),
        scratch_shapes=[pltpu.VMEM((HEAD_W, HEAD_W), F32)],
        compiler_params=_params("parallel", "arbitrary"),
        name=name,
    )(p, *consts)


def _lru_kernel(p_ref, cw_ref, cb_ref, wa_ref, ba_ref, wx_ref, bx_ref, lam_ref, out_ref,
                xpad_ref, h_ref):
    tt = p_ref.shape[0]
    pad = 8

    @pl.when(pl.program_id(1) == 0)
    def _():
        xpad_ref[0:pad, :] = jnp.zeros((pad, C_WIDTH), F32)
        h_ref[...] = jnp.zeros_like(h_ref)

    x = p_ref[:, 0:C_WIDTH]
    xpad_ref[pad:pad + tt, :] = x
    base = pad - (C_CONV - 1)
    xc = xpad_ref[base:base + tt, :] * cw_ref[0:1, :]
    for j in range(1, C_CONV):
        xc = xc + xpad_ref[base + j:base + j + tt, :] * cw_ref[j:j + 1, :]
    xc = xc + cb_ref[...]
    xpad_ref[0:pad, :] = x[tt - pad:tt, :]

    xb = xc.astype(BF16)
    r = jax.nn.sigmoid(_dot(xb, wa_ref[...]) + ba_ref[...])
    i = jax.nn.sigmoid(_dot(xb, wx_ref[...]) + bx_ref[...])
    lam = lam_ref[...]
    softplus_neg_lam = jnp.maximum(-lam, 0.0) + jnp.log1p(jnp.exp(-jnp.abs(lam)))
    log_a = -C_EXP * r * softplus_neg_lam
    a = jnp.exp(log_a)
    u = jnp.sqrt(jnp.maximum(-_expm1(2.0 * log_a), 0.0)) * (i * xc)

    rows = lax.broadcasted_iota(I32, (tt, C_WIDTH), 0)
    s = 1
    while s < tt:
        keep = rows >= s
        u = jnp.where(keep, a * pltpu.roll(u, s, 0) + u, u)
        a = jnp.where(keep, a * pltpu.roll(a, s, 0), a)
        s *= 2
    h = u + a * h_ref[0:1, :]
    h_ref[...] = jnp.broadcast_to(h[tt - 1:tt, :], h_ref.shape)
    out_ref[...] = h * jax.nn.gelu(p_ref[:, C_WIDTH:2 * C_WIDTH])


def _lru(pc, cw, cb, wa, ba, wx, bx, lam):
    b, t, w = pc.shape
    tt = min(TT_LRU, t)
    consts = [cw, cb, wa, ba, wx, bx, lam]
    return pl.pallas_call(
        _lru_kernel,
        grid=(b, t // tt),
        in_specs=[pl.BlockSpec((None, tt, w), lambda i, j: (i, j, 0))]
        + [_const_spec(c.shape) for c in consts],
        out_specs=pl.BlockSpec((None, tt, C_WIDTH), lambda i, j: (i, j, 0)),
        out_shape=jax.ShapeDtypeStruct((b, t, C_WIDTH), F32),
        scratch_shapes=[pltpu.VMEM((tt + 8, C_WIDTH), F32), pltpu.VMEM((8, C_WIDTH), F32)],
        compiler_params=_params("parallel", "arbitrary"),
        name="rglru",
    )(pc, *consts)


def _merge_kernel(x_ref, ya_ref, yb_ref, yc_ref, yd_ref, g_ref, wg_ref, bg_ref, wbr_ref, wo_ref,
                  out_ref):
    x = x_ref[...]
    d = x.shape[1]
    h = _rms(x, g_ref[...]).astype(BF16)
    merged = jnp.zeros(x.shape, F32)
    for n, y_ref in enumerate((ya_ref, yb_ref, yc_ref, yd_ref)):
        gate = jax.nn.sigmoid(_dot(h, wg_ref[:, n * d:(n + 1) * d]) + bg_ref[:, n * d:(n + 1) * d])
        merged = merged + gate * _dot(y_ref[...].astype(BF16), wbr_ref[n])
    out_ref[...] = x + _dot(merged.astype(BF16), wo_ref[...])


def _merge(x2, ys, g, wg, bg, wbr, wo):
    n, d = x2.shape
    tm = min(TM_PROJ, n)
    consts = [g, wg, bg, wbr, wo]
    return pl.pallas_call(
        _merge_kernel,
        grid=(n // tm,),
        in_specs=[pl.BlockSpec((tm, d), lambda i: (i, 0))]
        + [pl.BlockSpec((tm, HEAD_W), lambda i: (i, 0)) for _ in ys]
        + [_const_spec(c.shape) for c in consts],
        out_specs=pl.BlockSpec((tm, d), lambda i: (i, 0)),
        out_shape=jax.ShapeDtypeStruct((n, d), F32),
        compiler_params=_params("parallel"),
        name="merge",
    )(x2, *ys, *consts)


def _ffn_kernel(x_ref, g_ref, w1_ref, w2_ref, gf_ref, out_ref, *, final):
    x = x_ref[...]
    d = x.shape[1]
    h = _rms(x, g_ref[...]).astype(BF16)
    acc = x
    for c in range(w1_ref.shape[1] // d):
        hid = jnp.square(jnp.maximum(_dot(h, w1_ref[:, c * d:(c + 1) * d]), 0.0))
        acc = acc + _dot(hid.astype(BF16), w2_ref[c * d:(c + 1) * d, :])
    if final:
        acc = _rms(acc, gf_ref[...])
    out_ref[...] = acc


def _ffn(x2, g, w1, w2, gf, final):
    n, d = x2.shape
    tm = min(TM_PROJ, n)
    consts = [g, w1, w2, gf]
    return pl.pallas_call(
        functools.partial(_ffn_kernel, final=final),
        grid=(n // tm,),
        in_specs=[pl.BlockSpec((tm, d), lambda i: (i, 0))] + [_const_spec(c.shape) for c in consts],
        out_specs=pl.BlockSpec((tm, d), lambda i: (i, 0)),
        out_shape=jax.ShapeDtypeStruct((n, d), F32),
        compiler_params=_params("parallel"),
        name="ffn",
    )(x2, *consts)


def _w_in_relayout_kernel(w_ref, out_ref):
    w = w_ref[...]
    offs = np.concatenate([[0], np.cumsum(IN_SPLITS)])
    (a_q, a_ckv, a_iq, a_ik, a_iw, b_q, b_k, b_v, b_lr, b_r, c_x, c_y, d_q, d_f, d_i, d_g) = (
        w[:, offs[i]:offs[i + 1]] for i in range(len(IN_SPLITS)))
    pad = None
    parts = [a_q, a_iq, a_iw, pad,
             a_ckv, a_ik, a_ik, a_ik, a_ik,
             b_q, b_k, b_v, b_r, b_lr, pad,
             c_x, c_y, d_q, d_f, d_i, d_g]
    out_ref[...] = jnp.zeros(out_ref.shape, out_ref.dtype)
    o = 0
    for part in parts:
        if part is None:
            o = -(-o // LANES) * LANES
            continue
        out_ref[:, o:o + part.shape[1]] = part.astype(BF16)
        o += part.shape[1]
    assert o == IN_W


def _w_in_relayout(w):
    d, cols = w.shape
    rows = min(128, d)
    return pl.pallas_call(
        _w_in_relayout_kernel,
        grid=(d // rows,),
        in_specs=[pl.BlockSpec((rows, cols), lambda i: (i, 0))],
        out_specs=pl.BlockSpec((rows, IN_W), lambda i: (i, 0)),
        out_shape=jax.ShapeDtypeStruct((d, IN_W), BF16),
        compiler_params=_params("parallel"),
        name="w_in_relayout",
    )(w)


def _block_diag(w):
    n, bi, bj = w.shape
    eye = jnp.eye(n, dtype=w.dtype)
    return (w[:, :, None, :] * eye[:, None, :, None]).reshape(n * bi, n * bj)


def _wide_heads(w):
    h, c, d = w.shape
    eye = jnp.eye(h, dtype=w.dtype)
    return (w[:, :, None, :] * eye[:, None, :, None]).reshape(h, c, h * d)


def kernel(x, norm1_g, w_in, w_gate, b_gate, kv_norm_g, w_uk, w_uv, rel_bias, w_gk2, b_gk, gla_norm_g, conv_w, conv_b, w_rg_a, b_rg_a, w_rg_x, b_rg_x, lru_lambda, lb_param, hgrn_norm_g, w_br_a, w_br_b, w_br_c, w_br_d, w_out, norm2_g, w_ff1, w_ff2, final_norm_g):
    bsz, t, d = x.shape
    depth = w_in.shape[0]
    n = bsz * t
    row = lambda v: v.reshape(1, -1)
    bias_tiles = _bias_tiles(rel_bias)
    x2 = x.reshape(n, d)
    for l in range(depth):
        aq, ak, ckt, pb, pc, pd = _inproj(x2, row(norm1_g[l]), _w_in_relayout(w_in[l]), row(kv_norm_g[l]))
        to3 = lambda a: a.reshape(bsz, t, a.shape[-1])

        y_a = _dsa(to3(aq), to3(ak), ckt, w_uk[l].reshape(A_HEADS * A_HEAD_DIM, A_KV_LATENT).astype(BF16),
                   jnp.swapaxes(_wide_heads(w_uv[l]), 1, 2).astype(BF16), bias_tiles)

        wgk = jnp.concatenate([w_gk2[l], jnp.zeros((LANES - B_GATE_RANK, HEAD_W), F32)], axis=0)
        y_b = _gla_call(_gla_b_kernel, to3(pb),
                        [wgk.astype(BF16), row(b_gk[l]), row(jnp.tile(gla_norm_g[l], B_HEADS))], "gla")

        y_c = _lru(to3(pc), conv_w[l], row(conv_b[l]), _block_diag(w_rg_a[l]).astype(BF16),
                   row(b_rg_a[l]), _block_diag(w_rg_x[l]).astype(BF16), row(b_rg_x[l]),
                   row(lru_lambda[l]))

        y_d = _gla_call(functools.partial(_gla_d_kernel, layer=l), to3(pd),
                        [lb_param, row(jnp.tile(hgrn_norm_g[l], D_HEADS))], "hgrn2")

        ys = [y.reshape(n, HEAD_W) for y in (y_a, y_b, y_c, y_d)]
        wbr = jnp.stack([w_br_a[l], w_br_b[l], w_br_c[l], w_br_d[l]]).astype(BF16)
        x2 = _merge(x2, ys, row(norm1_g[l]), w_gate[l].astype(BF16), row(b_gate[l]), wbr,
                    w_out[l].astype(BF16))
        x2 = _ffn(x2, row(norm2_g[l]), w_ff1[l].astype(BF16), w_ff2[l].astype(BF16),
                  row(final_norm_g), final=(l == depth - 1))
    return x2.reshape(bsz, t, d)
```

```python
import functools
import math

import numpy as np
import jax
import jax.numpy as jnp
from jax import lax
from jax.experimental import pallas as pl
from jax.experimental.pallas import tpu as pltpu

F32 = jnp.float32
BF16 = jnp.bfloat16
I32 = jnp.int32

EPS = 1e-6
NEG_BIG = -1e30
UNSELECTED = -3e38
LB_FLOOR = 1e-20
INT_MIN = -(2 ** 31)
LOG2E = 1.4426950408889634

A_HEADS = 4
A_HEAD_DIM = 64
A_KV_LATENT = 128
IDX_HEADS = 4
IDX_DIM = 64
MAX_TOPK = 256
Q_BLOCK = 128
N_BUCKETS = 32
MAX_DISTANCE = 128
B_HEADS = 4
B_DK = 64
B_DV = 64
B_GATE_RANK = 16
B_GATE_TAU = 16.0
C_WIDTH = 256
C_BLOCKS = 4
C_CONV = 4
C_EXP = 8.0
D_HEADS = 4
D_DK = 64
D_DV = 64
CHUNK = 64
N_BRANCH = 4

HEAD_W = 256
LANES = 128

IN_SPLITS = (
    A_HEADS * A_HEAD_DIM, A_KV_LATENT, IDX_HEADS * IDX_DIM, IDX_DIM, IDX_HEADS,
    B_HEADS * B_DK, B_HEADS * B_DK, B_HEADS * B_DV, B_GATE_RANK, B_HEADS * B_DV,
    C_WIDTH, C_WIDTH,
    D_HEADS * D_DK, D_HEADS * D_DK, D_HEADS * D_DV, D_HEADS * D_DV,
)

AQ_W = 640
AK_W = 384
PB_W = 1152
PC_W = 512
PD_W = 1024
IN_W = AQ_W + AK_W + PB_W + PC_W + PD_W

VMEM_LIMIT = 56 * 1024 * 1024

TM_PROJ = 512
TT_GLA = 256
TT_LRU = 256
GLA_LEVELS = (32, 16, 8, 4, 2, 1)
GLA_MXU_LEVELS = (2, 1)


def _rms(x, g):
    return x * lax.rsqrt(jnp.mean(x * x, axis=-1, keepdims=True) + EPS) * g


def _dot(a, b):
    return jnp.dot(a, b, preferred_element_type=F32)


def _dot_nt(a, b):
    return lax.dot_general(a, b, (((1,), (1,)), ((), ())), preferred_element_type=F32)


def _split3(x):
    hi = x.astype(BF16)
    rest = x - hi.astype(F32)
    mid = rest.astype(BF16)
    lo = (rest - mid.astype(F32)).astype(BF16)
    return lo, mid, hi


def _stack_heads(a, lane_head):
    return jnp.concatenate([jnp.where(lane_head == h, a, 0.0) for h in range(4)], axis=0)


def _params(*sem):
    return pltpu.CompilerParams(dimension_semantics=sem, vmem_limit_bytes=VMEM_LIMIT)


def _const_spec(shape):
    nd = len(shape)
    return pl.BlockSpec(shape, lambda *_: (0,) * nd)


def _inproj_kernel(x_ref, g_ref, w_ref, kvg_ref, aq_ref, ak_ref, ckt_ref, pb_ref, pc_ref, pd_ref):
    h = _rms(x_ref[...], g_ref[...]).astype(BF16)
    o = 0
    aq_ref[...] = _dot(h, w_ref[:, o:o + AQ_W])
    o += AQ_W
    ak = _dot(h, w_ref[:, o:o + AK_W])
    ckv = _rms(ak[:, :A_KV_LATENT], kvg_ref[...])
    ak_ref[:, :A_KV_LATENT] = ckv.astype(BF16)
    ak_ref[:, A_KV_LATENT:] = ak[:, A_KV_LATENT:].astype(BF16)
    ckt_ref[0] = ckv.T.astype(BF16)
    o += AK_W
    pb_ref[...] = _dot(h, w_ref[:, o:o + PB_W])
    o += PB_W
    pc_ref[...] = _dot(h, w_ref[:, o:o + PC_W])
    o += PC_W
    pd_ref[...] = _dot(h, w_ref[:, o:o + PD_W])


def _inproj(x2, g, w, kvg):
    n, d = x2.shape
    tm = min(TM_PROJ, n)
    def rows(wd, dt):
        return pl.BlockSpec((tm, wd), lambda i: (i, 0)), jax.ShapeDtypeStruct((n, wd), dt)

    ckt = (pl.BlockSpec((1, A_KV_LATENT, tm), lambda i: (i, 0, 0)),
           jax.ShapeDtypeStruct((n // tm, A_KV_LATENT, tm), BF16))
    outs = [rows(AQ_W, F32), rows(AK_W, BF16), ckt, rows(PB_W, F32), rows(PC_W, F32), rows(PD_W, F32)]
    return pl.pallas_call(
        _inproj_kernel,
        grid=(n // tm,),
        in_specs=[pl.BlockSpec((tm, d), lambda i: (i, 0)), _const_spec((1, d)),
                  _const_spec((d, IN_W)), _const_spec((1, A_KV_LATENT))],
        out_specs=[o[0] for o in outs],
        out_shape=[o[1] for o in outs],
        compiler_params=_params("parallel"),
        name="inproj",
    )(x2, g, w, kvg)


def _t5_bucket(dist):
    n = jnp.maximum(dist, 0)
    max_exact = N_BUCKETS // 2
    nf = jnp.maximum(n, max_exact).astype(F32)
    large = max_exact + (jnp.log(nf / max_exact) / math.log(MAX_DISTANCE / max_exact)
                         * (N_BUCKETS - max_exact)).astype(I32)
    large = jnp.minimum(large, N_BUCKETS - 1)
    return jnp.where(n < max_exact, n, large)


def _bias_tiles_kernel(rb_ref, out_ref):
    krow = lax.broadcasted_iota(I32, (Q_BLOCK, Q_BLOCK), 0)
    qcol = lax.broadcasted_iota(I32, (Q_BLOCK, Q_BLOCK), 1)
    for d in range(3):
        bucket = _t5_bucket(d * Q_BLOCK + qcol - krow)
        for h in range(A_HEADS):
            acc = jnp.zeros((Q_BLOCK, Q_BLOCK), F32)
            for k in range(N_BUCKETS):
                acc = jnp.where(bucket == k, rb_ref[k, h], acc)
            out_ref[d, :, h * Q_BLOCK:(h + 1) * Q_BLOCK] = acc


def _bias_tiles(rel_bias):
    return pl.pallas_call(
        _bias_tiles_kernel,
        in_specs=[pl.BlockSpec(memory_space=pltpu.SMEM)],
        out_specs=pl.BlockSpec(memory_space=pltpu.VMEM),
        out_shape=jax.ShapeDtypeStruct((3, Q_BLOCK, A_HEADS * Q_BLOCK), F32),
        name="bias_tiles",
    )(rel_bias)


def _dsa_kernel(aq_ref, ak_ref, ckt_ref, wuk_ref, wuvt_ref, bias_ref, tri_ref, out_ref,
                keys_ref, s_ref, *, topk, ksb):
    qb = Q_BLOCK
    ng = ksb // qb
    qi = pl.program_id(1)
    nsb = qi // ng + 1
    aq = aq_ref[...]
    lane_head = lax.broadcasted_iota(I32, (qb, HEAD_W), 1) // A_HEAD_DIM
    krow = lax.broadcasted_iota(I32, (qb, qb), 0)
    qpos = qi * qb + lax.broadcasted_iota(I32, (qb, qb), 1)

    def causal(sb, g):
        return sb * ksb + g * qb + krow <= qpos

    groups = [(g, slice(g * qb, (g + 1) * qb)) for g in range(ng)]
    heads = [(h, slice(h * qb, (h + 1) * qb)) for h in range(A_HEADS)]

    def fold_rows(a, op, rows):
        return op(a.reshape(a.shape[0] // rows, rows, a.shape[1]), axis=0)

    qx = _stack_heads(aq[:, 256:512], lane_head).astype(BF16)
    qlat = _dot(_stack_heads(aq[:, 0:256], lane_head).astype(BF16), wuk_ref[...])
    qlat = (qlat * (A_HEAD_DIM ** -0.5)).astype(BF16)
    iw_t = (aq[:, 512:640] * (IDX_HEADS ** -0.5 * IDX_DIM ** -0.5)).T
    iw_rows = [iw_t[h:h + 1, :] for h in range(IDX_HEADS)]

    def score_body(sb, _):
        off = pl.multiple_of(sb * ksb, ksb)
        ik = ak_ref[pl.ds(off, ksb), A_KV_LATENT:]
        ck = ak_ref[pl.ds(off, ksb), :A_KV_LATENT]
        rel = jnp.maximum(_dot_nt(ik, qx), 0.0)
        st = _dot_nt(ck, qlat)
        for g, rs in groups:
            sc = jnp.zeros((qb, qb), F32)
            for h, cs in heads:
                sc = sc + rel[rs, cs] * iw_rows[h]
            sc = jnp.where(causal(sb, g), sc, NEG_BIG)
            bits = pltpu.bitcast(sc, I32)
            rows = pl.ds(off + g * qb, qb)
            keys_ref[rows, :] = bits ^ ((bits >> 31) & 0x7FFFFFFF)
            bias = bias_ref[jnp.clip(qi - (sb * ng + g), 0, 2)]
            s_ref[rows, :] = (st[rs] + bias) * LOG2E
        return 0

    lax.fori_loop(0, nsb, score_body, 0)

    acc_rows = 32

    def count(pred_fn):
        def body(sb, acc):
            off = pl.multiple_of(sb * ksb, ksb)
            hit = jnp.where(pred_fn(keys_ref[pl.ds(off, ksb), :]), 1.0, 0.0)
            return acc + fold_rows(hit, jnp.sum, acc_rows)
        acc = lax.fori_loop(0, nsb, body, jnp.zeros((acc_rows, qb), F32))
        return jnp.sum(acc, axis=0, keepdims=True)

    def bit_body(i, carry):
        theta, cnt_ge = carry
        cand = theta + jnp.left_shift(jnp.int32(1), 31 - i)
        cnt = count(lambda key: key >= cand)
        ok = cnt >= topk
        return jnp.where(ok, cand, theta), jnp.where(ok, cnt, cnt_ge)

    init = (jnp.full((1, qb), INT_MIN, I32), jnp.zeros((1, qb), F32) + (nsb * ksb).astype(F32))
    theta, cnt_ge = lax.fori_loop(0, 32, bit_body, init)
    need = topk - count(lambda key: key > theta)
    tie_break = jnp.max(cnt_ge) > topk

    ones = jnp.ones((A_KV_LATENT, ksb), BF16)

    def attend_body(ranked, sb, carry):
        m, acc, tie_seen = carry
        off = pl.multiple_of(sb * ksb, ksb)
        key = keys_ref[pl.ds(off, ksb), :]
        if ranked:
            eq = key == theta
            eqf = jnp.where(eq, 1.0, 0.0)
            rank = tie_seen + _dot(tri_ref[...], eqf.astype(BF16))
            tie_seen = rank[ksb - 1:ksb] + eqf[ksb - 1:ksb]
            sel = (key > theta) | (eq & (rank < need))
        else:
            sel = key >= theta
        blocks, mx = [], []
        for g, rs in groups:
            valid = sel[rs] & causal(sb, g)
            raw = s_ref[pl.ds(off + g * qb, qb), :]
            blocks.append(jnp.concatenate(
                [jnp.where(valid, raw[:, cs], UNSELECTED) for _, cs in heads], axis=1))
            mx.append(fold_rows(blocks[-1], jnp.max, 8))
        mx = jnp.max(functools.reduce(jnp.maximum, mx), axis=0, keepdims=True)
        m_new = jnp.maximum(m, mx)
        p = jnp.exp2(jnp.concatenate(blocks, axis=0) - m_new).astype(BF16)
        acc = acc * jnp.exp2(m - m_new) + _dot(jnp.concatenate([ckt_ref[sb], ones], axis=0), p)
        return m_new, acc, tie_seen

    def attend(ranked):
        init = (jnp.full((1, A_HEADS * qb), NEG_BIG, F32),
                jnp.zeros((2 * A_KV_LATENT, A_HEADS * qb), F32), jnp.zeros((1, qb), F32))
        return lax.fori_loop(0, nsb, functools.partial(attend_body, ranked), init)[1]

    acc = lax.cond(tie_break, lambda: attend(True), lambda: attend(False))
    o_t = (acc[:A_KV_LATENT] / acc[A_KV_LATENT:]).astype(BF16)
    y_t = jnp.zeros((HEAD_W, qb), F32)
    for h, cs in heads:
        y_t = y_t + _dot(wuvt_ref[h], o_t[:, cs])
    out_ref[...] = y_t.T


def _dsa(aq, ak, ckt, wuk, wuvt, bias_tiles):
    b, t, _ = aq.shape
    ksb = ckt.shape[2]
    assert t % ksb == 0 and ksb % Q_BLOCK == 0 and ckt.shape[0] * ksb == b * t
    nsb = t // ksb
    topk = min(MAX_TOPK, t // 4)
    idx = np.arange(ksb)
    tri = jnp.asarray(idx[None, :] < idx[:, None], BF16)
    return pl.pallas_call(
        functools.partial(_dsa_kernel, topk=topk, ksb=ksb),
        grid=(b, t // Q_BLOCK),
        in_specs=[pl.BlockSpec((None, Q_BLOCK, AQ_W), lambda i, j: (i, j, 0)),
                  pl.BlockSpec((None, t, AK_W), lambda i, j: (i, 0, 0)),
                  pl.BlockSpec((nsb, A_KV_LATENT, ksb), lambda i, j: (i, 0, 0)),
                  _const_spec(wuk.shape), _const_spec(wuvt.shape), _const_spec(bias_tiles.shape),
                  _const_spec(tri.shape)],
        out_specs=pl.BlockSpec((None, Q_BLOCK, HEAD_W), lambda i, j: (i, j, 0)),
        out_shape=jax.ShapeDtypeStruct((b, t, HEAD_W), F32),
        scratch_shapes=[pltpu.VMEM((t, Q_BLOCK), I32), pltpu.VMEM((t, A_HEADS * Q_BLOCK), F32)],
        compiler_params=_params("parallel", "arbitrary"),
        name="dsa",
    )(aq, ak, ckt, wuk, wuvt, bias_tiles, tri)


def _gla_static_tables():
    c = CHUNK
    idx = np.arange(c)
    tri = (idx[None, :] <= idx[:, None]).astype(np.float32)
    blocks = [tri]
    for s in GLA_MXU_LEVELS:
        end_left = (idx // (2 * s)) * 2 * s + s - 1
        blocks.append(tri[end_left])
    lev = np.full((c, c), -1, np.int32)
    for i in range(c):
        for j in range(c):
            if i == j:
                lev[i, j] = len(GLA_LEVELS)
            elif i > j:
                s = 1 << int(math.floor(math.log2(i ^ j)))
                lev[i, j] = GLA_LEVELS.index(s)
    pool = np.kron(np.eye(4, dtype=np.float32), np.full((64, 64), 1.0 / 64, np.float32))
    return np.concatenate(blocks, axis=0), np.tile(lev, (1, 4)), pool


def _expm1(x):
    return jnp.tanh(0.5 * x) * (jnp.exp(x) + 1.0)


def _log_sigmoid(x):
    return -(jnp.maximum(-x, 0.0) + jnp.log1p(jnp.exp(-jnp.abs(x))))


def _gla_core(q, k, v, g, mall, lev, st):
    c = CHUNK
    lane_head = lax.broadcasted_iota(I32, (c, HEAD_W), 1) // 64
    rowc = lax.broadcasted_iota(I32, (c, HEAD_W), 0)
    bc = sum(_dot(mall, part) for part in _split3(g))
    b = bc[0:c]
    blast = b[c - 1:c, :]
    att = jnp.zeros((c, HEAD_W), F32)
    for li, s in enumerate(GLA_LEVELS):
        if s in GLA_MXU_LEVELS:
            at = GLA_MXU_LEVELS.index(s) + 1
            mid = bc[at * c:(at + 1) * c]
        else:
            mid = jnp.concatenate([jnp.broadcast_to(b[p + s - 1:p + s, :], (2 * s, HEAD_W))
                                   for p in range(0, c, 2 * s)], axis=0)
        right = (rowc & s) != 0
        qt = jnp.where(right, q * jnp.exp(jnp.minimum(b - mid, 0.0)), 0.0)
        kt = jnp.where(right, 0.0, k * jnp.exp(jnp.minimum(mid - b, 0.0)))
        al = _dot_nt(qt.astype(BF16), _stack_heads(kt, lane_head).astype(BF16))
        att = jnp.where(lev == li, al, att)
    ad = _dot_nt(q.astype(BF16), _stack_heads(k, lane_head).astype(BF16))
    att = jnp.where(lev == len(GLA_LEVELS), ad, att)
    o = _dot(att.astype(BF16), _stack_heads(v, lane_head).astype(BF16))
    o = o + _dot_nt((q * jnp.exp(b)).astype(BF16), st.astype(BF16))
    ktail = k * jnp.exp(blast - b)
    upd = _dot(v.T.astype(BF16), ktail.astype(BF16))
    r2 = lax.broadcasted_iota(I32, (HEAD_W, HEAD_W), 0) // 64
    c2 = lax.broadcasted_iota(I32, (HEAD_W, HEAD_W), 1) // 64
    st = st * jnp.exp(blast) + jnp.where(r2 == c2, upd, 0.0)
    return o, st


def _gla_finish(o, r, ng, pool):
    ms = sum(_dot(part, pool) for part in _split3(o * o))
    return o * lax.rsqrt(ms + EPS) * ng * jax.nn.silu(r)


def _gla_b_kernel(p_ref, wgk_ref, bgk_ref, ng_ref, mall_ref, lev_ref, pool_ref, out_ref, st_ref):
    @pl.when(pl.program_id(1) == 0)
    def _():
        st_ref[...] = jnp.zeros_like(st_ref)

    tile = p_ref[...]
    q = tile[:, 0:256] * (B_DK ** -0.5)
    k = tile[:, 256:512]
    v = tile[:, 512:768]
    r = tile[:, 768:1024]
    z = _dot(tile[:, 1024:1152].astype(BF16), wgk_ref[...]) + bgk_ref[...]
    g = _log_sigmoid(z) / B_GATE_TAU
    mall, lev, st = mall_ref[...], lev_ref[...], st_ref[...]
    outs = []
    for c in range(tile.shape[0] // CHUNK):
        sl = slice(c * CHUNK, (c + 1) * CHUNK)
        o, st = _gla_core(q[sl], k[sl], v[sl], g[sl], mall, lev, st)
        outs.append(o)
    st_ref[...] = st
    out_ref[...] = _gla_finish(jnp.concatenate(outs, axis=0), r, ng_ref[...], pool_ref[...])


def _gla_d_kernel(p_ref, lbp_ref, ng_ref, mall_ref, lev_ref, pool_ref, out_ref, st_ref, *, layer):
    @pl.when(pl.program_id(1) == 0)
    def _():
        st_ref[...] = jnp.zeros_like(st_ref)

    lbp = lbp_ref[...]
    e = jnp.exp(lbp - jnp.max(lbp, axis=0, keepdims=True))
    soft = e / jnp.sum(e, axis=0, keepdims=True)
    cum = soft[0:1]
    for i in range(1, layer + 1):
        cum = cum + soft[i:i + 1]
    lb = cum - soft[0:1]

    tile = p_ref[...]
    q = jax.nn.silu(tile[:, 0:256])
    v = tile[:, 512:768]
    r = tile[:, 768:1024]
    t1 = jnp.log(jnp.maximum(lb, LB_FLOOR))
    t2 = jnp.log1p(-lb) + _log_sigmoid(tile[:, 256:512])
    g = jnp.maximum(t1, t2) + jnp.log1p(jnp.exp(-jnp.abs(t1 - t2)))
    k = -_expm1(g)
    mall, lev, st = mall_ref[...], lev_ref[...], st_ref[...]
    outs = []
    for c in range(tile.shape[0] // CHUNK):
        sl = slice(c * CHUNK, (c + 1) * CHUNK)
        o, st = _gla_core(q[sl], k[sl], v[sl], g[sl], mall, lev, st)
        outs.append(o)
    st_ref[...] = st
    out_ref[...] = _gla_finish(jnp.concatenate(outs, axis=0), r, ng_ref[...], pool_ref[...])


def _gla_call(kernel, p, extra, name):
    b, t, w = p.shape
    tt = min(TT_GLA, t)
    mall, lev, pool = _gla_static_tables()
    consts = list(extra) + [jnp.asarray(mall, BF16), jnp.asarray(lev), jnp.asarray(pool, BF16)]
    return pl.pallas_call(
        kernel,
        grid=(b, t // tt),
        in_specs=[pl.BlockSpec((None, tt, w), lambda i, j: (i, j, 0))]
        + [_const_spec(c.shape) for c in consts],
        out_specs=pl.BlockSpec((None, tt, HEAD_W), lambda i, j: (i, j, 0)),
        out_shape=jax.ShapeDtypeStruct((b, t, HEAD_W), F32),
        scratch_shapes=[pltpu.VMEM((HEAD_W, HEAD_W), F32)],
        compiler_params=_params("parallel", "arbitrary"),
        name=name,
    )(p, *consts)


def _lru_kernel(p_ref, cw_ref, cb_ref, wa_ref, ba_ref, wx_ref, bx_ref, lam_ref, out_ref,
                xpad_ref, h_ref):
    tt = p_ref.shape[0]
    pad = 8

    @pl.when(pl.program_id(1) == 0)
    def _():
        xpad_ref[0:pad, :] = jnp.zeros((pad, C_WIDTH), F32)
        h_ref[...] = jnp.zeros_like(h_ref)

    x = p_ref[:, 0:C_WIDTH]
    xpad_ref[pad:pad + tt, :] = x
    base = pad - (C_CONV - 1)
    xc = xpad_ref[base:base + tt, :] * cw_ref[0:1, :]
    for j in range(1, C_CONV):
        xc = xc + xpad_ref[base + j:base + j + tt, :] * cw_ref[j:j + 1, :]
    xc = xc + cb_ref[...]
    xpad_ref[0:pad, :] = x[tt - pad:tt, :]

    xb = xc.astype(BF16)
    r = jax.nn.sigmoid(_dot(xb, wa_ref[...]) + ba_ref[...])
    i = jax.nn.sigmoid(_dot(xb, wx_ref[...]) + bx_ref[...])
    lam = lam_ref[...]
    softplus_neg_lam = jnp.maximum(-lam, 0.0) + jnp.log1p(jnp.exp(-jnp.abs(lam)))
    log_a = -C_EXP * r * softplus_neg_lam
    a = jnp.exp(log_a)
    u = jnp.sqrt(jnp.maximum(-_expm1(2.0 * log_a), 0.0)) * (i * xc)

    rows = lax.broadcasted_iota(I32, (tt, C_WIDTH), 0)
    s = 1
    while s < tt:
        keep = rows >= s
        u = jnp.where(keep, a * pltpu.roll(u, s, 0) + u, u)
        a = jnp.where(keep, a * pltpu.roll(a, s, 0), a)
        s *= 2
    h = u + a * h_ref[0:1, :]
    h_ref[...] = jnp.broadcast_to(h[tt - 1:tt, :], h_ref.shape)
    out_ref[...] = h * jax.nn.gelu(p_ref[:, C_WIDTH:2 * C_WIDTH])


def _lru(pc, cw, cb, wa, ba, wx, bx, lam):
    b, t, w = pc.shape
    tt = min(TT_LRU, t)
    consts = [cw, cb, wa, ba, wx, bx, lam]
    return pl.pallas_call(
        _lru_kernel,
        grid=(b, t // tt),
        in_specs=[pl.BlockSpec((None, tt, w), lambda i, j: (i, j, 0))]
        + [_const_spec(c.shape) for c in consts],
        out_specs=pl.BlockSpec((None, tt, C_WIDTH), lambda i, j: (i, j, 0)),
        out_shape=jax.ShapeDtypeStruct((b, t, C_WIDTH), F32),
        scratch_shapes=[pltpu.VMEM((tt + 8, C_WIDTH), F32), pltpu.VMEM((8, C_WIDTH), F32)],
        compiler_params=_params("parallel", "arbitrary"),
        name="rglru",
    )(pc, *consts)


def _merge_kernel(x_ref, ya_ref, yb_ref, yc_ref, yd_ref, g_ref, wg_ref, bg_ref, wbr_ref, wo_ref,
                  out_ref):
    x = x_ref[...]
    d = x.shape[1]
    h = _rms(x, g_ref[...]).astype(BF16)
    merged = jnp.zeros(x.shape, F32)
    for n, y_ref in enumerate((ya_ref, yb_ref, yc_ref, yd_ref)):
        gate = jax.nn.sigmoid(_dot(h, wg_ref[:, n * d:(n + 1) * d]) + bg_ref[:, n * d:(n + 1) * d])
        merged = merged + gate * _dot(y_ref[...].astype(BF16), wbr_ref[n])
    out_ref[...] = x + _dot(merged.astype(BF16), wo_ref[...])


def _merge(x2, ys, g, wg, bg, wbr, wo):
    n, d = x2.shape
    tm = min(TM_PROJ, n)
    consts = [g, wg, bg, wbr, wo]
    return pl.pallas_call(
        _merge_kernel,
        grid=(n // tm,),
        in_specs=[pl.BlockSpec((tm, d), lambda i: (i, 0))]
        + [pl.BlockSpec((tm, HEAD_W), lambda i: (i, 0)) for _ in ys]
        + [_const_spec(c.shape) for c in consts],
        out_specs=pl.BlockSpec((tm, d), lambda i: (i, 0)),
        out_shape=jax.ShapeDtypeStruct((n, d), F32),
        compiler_params=_params("parallel"),
        name="merge",
    )(x2, *ys, *consts)


def _ffn_kernel(x_ref, g_ref, w1_ref, w2_ref, gf_ref, out_ref, *, final):
    x = x_ref[...]
    d = x.shape[1]
    h = _rms(x, g_ref[...]).astype(BF16)
    acc = x
    for c in range(w1_ref.shape[1] // d):
        hid = jnp.square(jnp.maximum(_dot(h, w1_ref[:, c * d:(c + 1) * d]), 0.0))
        acc = acc + _dot(hid.astype(BF16), w2_ref[c * d:(c + 1) * d, :])
    if final:
        acc = _rms(acc, gf_ref[...])
    out_ref[...] = acc


def _ffn(x2, g, w1, w2, gf, final):
    n, d = x2.shape
    tm = min(TM_PROJ, n)
    consts = [g, w1, w2, gf]
    return pl.pallas_call(
        functools.partial(_ffn_kernel, final=final),
        grid=(n // tm,),
        in_specs=[pl.BlockSpec((tm, d), lambda i: (i, 0))] + [_const_spec(c.shape) for c in consts],
        out_specs=pl.BlockSpec((tm, d), lambda i: (i, 0)),
        out_shape=jax.ShapeDtypeStruct((n, d), F32),
        compiler_params=_params("parallel"),
        name="ffn",
    )(x2, *consts)


def _w_in_relayout_kernel(w_ref, out_ref):
    w = w_ref[...]
    offs = np.concatenate([[0], np.cumsum(IN_SPLITS)])
    (a_q, a_ckv, a_iq, a_ik, a_iw, b_q, b_k, b_v, b_lr, b_r, c_x, c_y, d_q, d_f, d_i, d_g) = (
        w[:, offs[i]:offs[i + 1]] for i in range(len(IN_SPLITS)))
    pad = None
    parts = [a_q, a_iq, a_iw, pad,
             a_ckv, a_ik, a_ik, a_ik, a_ik,
             b_q, b_k, b_v, b_r, b_lr, pad,
             c_x, c_y, d_q, d_f, d_i, d_g]
    out_ref[...] = jnp.zeros(out_ref.shape, out_ref.dtype)
    o = 0
    for part in parts:
        if part is None:
            o = -(-o // LANES) * LANES
            continue
        out_ref[:, o:o + part.shape[1]] = part.astype(BF16)
        o += part.shape[1]
    assert o == IN_W


def _w_in_relayout(w):
    d, cols = w.shape
    rows = min(128, d)
    return pl.pallas_call(
        _w_in_relayout_kernel,
        grid=(d // rows,),
        in_specs=[pl.BlockSpec((rows, cols), lambda i: (i, 0))],
        out_specs=pl.BlockSpec((rows, IN_W), lambda i: (i, 0)),
        out_shape=jax.ShapeDtypeStruct((d, IN_W), BF16),
        compiler_params=_params("parallel"),
        name="w_in_relayout",
    )(w)


def _block_diag(w):
    n, bi, bj = w.shape
    eye = jnp.eye(n, dtype=w.dtype)
    return (w[:, :, None, :] * eye[:, None, :, None]).reshape(n * bi, n * bj)


def _wide_heads(w):
    h, c, d = w.shape
    eye = jnp.eye(h, dtype=w.dtype)
    return (w[:, :, None, :] * eye[:, None, :, None]).reshape(h, c, h * d)


def kernel(x, norm1_g, w_in, w_gate, b_gate, kv_norm_g, w_uk, w_uv, rel_bias, w_gk2, b_gk, gla_norm_g, conv_w, conv_b, w_rg_a, b_rg_a, w_rg_x, b_rg_x, lru_lambda, lb_param, hgrn_norm_g, w_br_a, w_br_b, w_br_c, w_br_d, w_out, norm2_g, w_ff1, w_ff2, final_norm_g):
    bsz, t, d = x.shape
    depth = w_in.shape[0]
    n = bsz * t
    row = lambda v: v.reshape(1, -1)
    bias_tiles = _bias_tiles(rel_bias)
    x2 = x.reshape(n, d)
    for l in range(depth):
        aq, ak, ckt, pb, pc, pd = _inproj(x2, row(norm1_g[l]), _w_in_relayout(w_in[l]), row(kv_norm_g[l]))
        to3 = lambda a: a.reshape(bsz, t, a.shape[-1])

        y_a = _dsa(to3(aq), to3(ak), ckt, w_uk[l].reshape(A_HEADS * A_HEAD_DIM, A_KV_LATENT).astype(BF16),
                   jnp.swapaxes(_wide_heads(w_uv[l]), 1, 2).astype(BF16), bias_tiles)

        wgk = jnp.concatenate([w_gk2[l], jnp.zeros((LANES - B_GATE_RANK, HEAD_W), F32)], axis=0)
        y_b = _gla_call(_gla_b_kernel, to3(pb),
                        [wgk.astype(BF16), row(b_gk[l]), row(jnp.tile(gla_norm_g[l], B_HEADS))], "gla")

        y_c = _lru(to3(pc), conv_w[l], row(conv_b[l]), _block_diag(w_rg_a[l]).astype(BF16),
                   row(b_rg_a[l]), _block_diag(w_rg_x[l]).astype(BF16), row(b_rg_x[l]),
                   row(lru_lambda[l]))

        y_d = _gla_call(functools.partial(_gla_d_kernel, layer=l), to3(pd),
                        [lb_param, row(jnp.tile(hgrn_norm_g[l], D_HEADS))], "hgrn2")

        ys = [y.reshape(n, HEAD_W) for y in (y_a, y_b, y_c, y_d)]
        wbr = jnp.stack([w_br_a[l], w_br_b[l], w_br_c[l], w_br_d[l]]).astype(BF16)
        x2 = _merge(x2, ys, row(norm1_g[l]), w_gate[l].astype(BF16), row(b_gate[l]), wbr,
                    w_out[l].astype(BF16))
        x2 = _ffn(x2, row(norm2_g[l]), w_ff1[l].astype(BF16), w_ff2[l].astype(BF16),
                  row(final_norm_g), final=(l == depth - 1))
    return x2.reshape(bsz, t, d)
```

```python
import functools
import math

import numpy as np
import jax
import jax.numpy as jnp
from jax import lax
from jax.experimental import pallas as pl
from jax.experimental.pallas import tpu as pltpu

F32 = jnp.float32
BF16 = jnp.bfloat16
I32 = jnp.int32

EPS = 1e-6
NEG_BIG = -1e30
UNSELECTED = -3e38
LB_FLOOR = 1e-20
INT_MIN = -(2 ** 31)
LOG2E = 1.4426950408889634

A_HEADS = 4
A_HEAD_DIM = 64
A_KV_LATENT = 128
IDX_HEADS = 4
IDX_DIM = 64
MAX_TOPK = 256
Q_BLOCK = 128
Q_STEP = 256
SOFTMAX_DENOM_FLOOR = 2.0 ** -40
N_BUCKETS = 32
MAX_DISTANCE = 128
B_HEADS = 4
B_DK = 64
B_DV = 64
B_GATE_RANK = 16
B_GATE_TAU = 16.0
C_WIDTH = 256
C_BLOCKS = 4
C_CONV = 4
C_EXP = 8.0
D_HEADS = 4
D_DK = 64
D_DV = 64
CHUNK = 64
N_BRANCH = 4

HEAD_W = 256
LANES = 128

IN_SPLITS = (
    A_HEADS * A_HEAD_DIM, A_KV_LATENT, IDX_HEADS * IDX_DIM, IDX_DIM, IDX_HEADS,
    B_HEADS * B_DK, B_HEADS * B_DK, B_HEADS * B_DV, B_GATE_RANK, B_HEADS * B_DV,
    C_WIDTH, C_WIDTH,
    D_HEADS * D_DK, D_HEADS * D_DK, D_HEADS * D_DV, D_HEADS * D_DV,
)

AQ_W = 640
AK_W = 384
PB_W = 1152
PC_W = 512
PD_W = 1024
IN_W = AQ_W + AK_W + PB_W + PC_W + PD_W

VMEM_LIMIT = 56 * 1024 * 1024

TM_PROJ = 512
TT_GLA = 256
TT_LRU = 256
GLA_LEVELS = (32, 16, 8, 4, 2, 1)
GLA_MXU_LEVELS = (2, 1)


def _rms(x, g):
    return x * lax.rsqrt(jnp.mean(x * x, axis=-1, keepdims=True) + EPS) * g


def _dot(a, b):
    return jnp.dot(a, b, preferred_element_type=F32)


def _dot_nt(a, b):
    return lax.dot_general(a, b, (((1,), (1,)), ((), ())), preferred_element_type=F32)


def _split3(x):
    hi = x.astype(BF16)
    rest = x - hi.astype(F32)
    mid = rest.astype(BF16)
    lo = (rest - mid.astype(F32)).astype(BF16)
    return lo, mid, hi


def _stack_heads(a, lane_head):
    return jnp.concatenate([jnp.where(lane_head == h, a, 0.0) for h in range(4)], axis=0)


def _params(*sem):
    return pltpu.CompilerParams(dimension_semantics=sem, vmem_limit_bytes=VMEM_LIMIT)


def _const_spec(shape):
    nd = len(shape)
    return pl.BlockSpec(shape, lambda *_: (0,) * nd)


def _inproj_kernel(x_ref, g_ref, w_ref, kvg_ref, aq_ref, ak_ref, ckt_ref, pb_ref, pc_ref, pd_ref):
    h = _rms(x_ref[...], g_ref[...]).astype(BF16)
    o = 0
    aq_ref[...] = _dot(h, w_ref[:, o:o + AQ_W])
    o += AQ_W
    ak = _dot(h, w_ref[:, o:o + AK_W])
    ckv = _rms(ak[:, :A_KV_LATENT], kvg_ref[...])
    ak_ref[:, :A_KV_LATENT] = ckv.astype(BF16)
    ak_ref[:, A_KV_LATENT:] = ak[:, A_KV_LATENT:].astype(BF16)
    ckt_ref[0] = ckv.T.astype(BF16)
    o += AK_W
    pb_ref[...] = _dot(h, w_ref[:, o:o + PB_W])
    o += PB_W
    pc_ref[...] = _dot(h, w_ref[:, o:o + PC_W])
    o += PC_W
    pd_ref[...] = _dot(h, w_ref[:, o:o + PD_W])


def _inproj(x2, g, w, kvg):
    n, d = x2.shape
    tm = min(TM_PROJ, n)
    def rows(wd, dt):
        return pl.BlockSpec((tm, wd), lambda i: (i, 0)), jax.ShapeDtypeStruct((n, wd), dt)

    ckt = (pl.BlockSpec((1, A_KV_LATENT, tm), lambda i: (i, 0, 0)),
           jax.ShapeDtypeStruct((n // tm, A_KV_LATENT, tm), BF16))
    outs = [rows(AQ_W, F32), rows(AK_W, BF16), ckt, rows(PB_W, F32), rows(PC_W, F32), rows(PD_W, F32)]
    return pl.pallas_call(
        _inproj_kernel,
        grid=(n // tm,),
        in_specs=[pl.BlockSpec((tm, d), lambda i: (i, 0)), _const_spec((1, d)),
                  _const_spec((d, IN_W)), _const_spec((1, A_KV_LATENT))],
        out_specs=[o[0] for o in outs],
        out_shape=[o[1] for o in outs],
        compiler_params=_params("parallel"),
        name="inproj",
    )(x2, g, w, kvg)


def _t5_bucket(dist):
    n = jnp.maximum(dist, 0)
    max_exact = N_BUCKETS // 2
    nf = jnp.maximum(n, max_exact).astype(F32)
    large = max_exact + (jnp.log(nf / max_exact) / math.log(MAX_DISTANCE / max_exact)
                         * (N_BUCKETS - max_exact)).astype(I32)
    large = jnp.minimum(large, N_BUCKETS - 1)
    return jnp.where(n < max_exact, n, large)


def _bias_tiles_kernel(rb_ref, out_ref):
    krow = lax.broadcasted_iota(I32, (Q_BLOCK, Q_BLOCK), 0)
    qcol = lax.broadcasted_iota(I32, (Q_BLOCK, Q_BLOCK), 1)
    for d in range(3):
        bucket = _t5_bucket(d * Q_BLOCK + qcol - krow)
        for h in range(A_HEADS):
            acc = jnp.zeros((Q_BLOCK, Q_BLOCK), F32)
            for k in range(N_BUCKETS):
                acc = jnp.where(bucket == k, rb_ref[k, h], acc)
            out_ref[d, :, h * Q_BLOCK:(h + 1) * Q_BLOCK] = acc


def _bias_tiles(rel_bias):
    return pl.pallas_call(
        _bias_tiles_kernel,
        in_specs=[pl.BlockSpec(memory_space=pltpu.SMEM)],
        out_specs=pl.BlockSpec(memory_space=pltpu.VMEM),
        out_shape=jax.ShapeDtypeStruct((3, Q_BLOCK, A_HEADS * Q_BLOCK), F32),
        name="bias_tiles",
    )(rel_bias)


def _dsa_kernel(aq_ref, ak_ref, ckt_ref, wuk_ref, wuvt_ref, bias_ref, tri_ref, out_ref,
                keys_ref, s_ref, acc_ref, *, topk, ksb, nu):
    kg = Q_BLOCK
    qw = nu * kg
    ng = ksb // kg
    qi = pl.program_id(1)
    nsb = ((qi + 1) * nu - 1) // ng + 1
    aq = aq_ref[...]
    lane_head = lax.broadcasted_iota(I32, (qw, HEAD_W), 1) // A_HEAD_DIM
    krow = lax.broadcasted_iota(I32, (kg, kg), 0)
    qcol = lax.broadcasted_iota(I32, (kg, kg), 1)

    def causal(sb, g, u):
        return sb * ksb + g * kg + krow <= (qi * nu + u) * kg + qcol

    groups = [(g, slice(g * kg, (g + 1) * kg)) for g in range(ng)]
    heads = [(h, slice(h * qw, (h + 1) * qw)) for h in range(A_HEADS)]
    blocks_u = [(u, slice(u * kg, (u + 1) * kg)) for u in range(nu)]

    def lanes(h, u):
        return slice(h * qw + u * kg, h * qw + (u + 1) * kg)

    def fold_rows(a, op, rows):
        return op(a.reshape(a.shape[0] // rows, rows, a.shape[1]), axis=0)

    qx = _stack_heads(aq[:, 256:512], lane_head).astype(BF16)
    qlat = _dot(_stack_heads(aq[:, 0:256], lane_head).astype(BF16), wuk_ref[...])
    qlat = (qlat * (A_HEAD_DIM ** -0.5)).astype(BF16)
    iw_t = (aq[:, 512:640] * (IDX_HEADS ** -0.5 * IDX_DIM ** -0.5)).T
    iw_rows = [iw_t[h:h + 1, :] for h in range(IDX_HEADS)]

    def score_body(sb, top):
        off = pl.multiple_of(sb * ksb, ksb)
        ik = ak_ref[pl.ds(off, ksb), A_KV_LATENT:]
        ck = ak_ref[pl.ds(off, ksb), :A_KV_LATENT]
        rel = jnp.maximum(_dot_nt(ik, qx), 0.0)
        st = _dot_nt(ck, qlat)
        for g, rs in groups:
            rows = pl.ds(off + g * kg, kg)
            sc = jnp.zeros((kg, qw), F32)
            for h, cs in heads:
                sc = sc + rel[rs, cs] * iw_rows[h]
            peak = {}
            for u, us in blocks_u:
                bits = pltpu.bitcast(jnp.where(causal(sb, g, u), sc[:, us], NEG_BIG), I32)
                keys_ref[rows, us] = bits ^ ((bits >> 31) & 0x7FFFFFFF)
                bias = bias_ref[jnp.clip(qi * nu + u - (sb * ng + g), 0, 2)]
                for h, _ in heads:
                    logit = (st[rs, lanes(h, u)] + bias[:, h * kg:(h + 1) * kg]) * LOG2E
                    s_ref[rows, lanes(h, u)] = logit
                    peak[h, u] = fold_rows(logit, jnp.max, 8)
            top = jnp.maximum(top, jnp.concatenate(
                [peak[h, u] for h, _ in heads for u, _ in blocks_u], axis=1))
        return top

    top = lax.fori_loop(0, nsb, score_body, jnp.full((8, A_HEADS * qw), NEG_BIG, F32))
    m_bound = jnp.max(top, axis=0, keepdims=True)

    acc_rows = 32

    def count(pred_fn):
        def body(sb, acc):
            off = pl.multiple_of(sb * ksb, ksb)
            hit = jnp.where(pred_fn(keys_ref[pl.ds(off, ksb), :]), 1.0, 0.0)
            return acc + fold_rows(hit, jnp.sum, acc_rows)
        acc = lax.fori_loop(0, nsb, body, jnp.zeros((acc_rows, qw), F32))
        return jnp.sum(acc, axis=0, keepdims=True)

    def bit_body(i, carry):
        theta, cnt_ge = carry
        cand = theta + jnp.left_shift(jnp.int32(1), 31 - i)
        cnt = count(lambda key: key >= cand)
        ok = cnt >= topk
        return jnp.where(ok, cand, theta), jnp.where(ok, cnt, cnt_ge)

    init = (jnp.full((1, qw), INT_MIN, I32), jnp.zeros((1, qw), F32) + (nsb * ksb).astype(F32))
    theta, cnt_ge = lax.fori_loop(0, 32, bit_body, init)
    need = topk - count(lambda key: key > theta)
    tie_break = jnp.max(cnt_ge) > topk

    ones = jnp.ones((A_KV_LATENT, ksb), BF16)

    def masked_logits(ranked, sb, tie_seen):
        off = pl.multiple_of(sb * ksb, ksb)
        key = keys_ref[pl.ds(off, ksb), :]
        if ranked:
            eqf = jnp.where(key == theta, 1.0, 0.0)
            rank = tie_seen + _dot(tri_ref[...], eqf.astype(BF16))
            tie_seen = rank[ksb - 1:ksb] + eqf[ksb - 1:ksb]

        def selected(rs, us):
            k, th = key[rs, us], theta[:, us]
            if ranked:
                return (k > th) | ((k == th) & (rank[rs, us] < need[:, us]))
            return k >= th

        out = []
        for g, rs in groups:
            raw = s_ref[pl.ds(off + g * kg, kg), :]
            valid = [selected(rs, us) & causal(sb, g, u) for u, us in blocks_u]
            out.append(jnp.concatenate(
                [jnp.where(valid[u], raw[:, lanes(h, u)], UNSELECTED)
                 for h, _ in heads for u, _ in blocks_u], axis=1))
        return jnp.concatenate(out, axis=0), tie_seen

    def accumulate(sb, p, scale=None):
        new = _dot(jnp.concatenate([ckt_ref[sb], ones], axis=0), p.astype(BF16))
        acc_ref[...] = new + (acc_ref[...] if scale is None else acc_ref[...] * scale)

    def attend_online(ranked):
        acc_ref[...] = jnp.zeros_like(acc_ref)

        def body(sb, carry):
            m, tie_seen = carry
            logits, tie_seen = masked_logits(ranked, sb, tie_seen)
            m_new = jnp.maximum(m, jnp.max(fold_rows(logits, jnp.max, 8), axis=0, keepdims=True))
            accumulate(sb, jnp.exp2(logits - m_new), jnp.exp2(m - m_new))
            return m_new, tie_seen

        lax.fori_loop(0, nsb, body,
                      (jnp.full((1, A_HEADS * qw), NEG_BIG, F32), jnp.zeros((1, qw), F32)))

    def attend_bounded():
        acc_ref[...] = jnp.zeros_like(acc_ref)

        def body(sb, _):
            logits, _ = masked_logits(False, sb, None)
            accumulate(sb, jnp.exp2(logits - m_bound))
            return 0

        lax.fori_loop(0, nsb, body, 0)
        smallest = jnp.min(acc_ref[A_KV_LATENT:A_KV_LATENT + 1, :])

        @pl.when(jnp.logical_not(smallest > SOFTMAX_DENOM_FLOOR))
        def _():
            attend_online(False)

    lax.cond(tie_break, lambda: attend_online(True), attend_bounded)
    acc = acc_ref[...]
    o_t = (acc[:A_KV_LATENT] / acc[A_KV_LATENT:]).astype(BF16)
    y_t = jnp.zeros((HEAD_W, qw), F32)
    for h, cs in heads:
        y_t = y_t + _dot(wuvt_ref[h], o_t[:, cs])
    out_ref[...] = y_t.T


def _dsa(aq, ak, ckt, wuk, wuvt, bias_tiles):
    b, t, _ = aq.shape
    ksb = ckt.shape[2]
    nu = Q_STEP // Q_BLOCK
    assert t % ksb == 0 and ksb % Q_BLOCK == 0 and ckt.shape[0] * ksb == b * t and t % Q_STEP == 0
    nsb = t // ksb
    topk = min(MAX_TOPK, t // 4)
    idx = np.arange(ksb)
    tri = jnp.asarray(idx[None, :] < idx[:, None], BF16)
    return pl.pallas_call(
        functools.partial(_dsa_kernel, topk=topk, ksb=ksb, nu=nu),
        grid=(b, t // Q_STEP),
        in_specs=[pl.BlockSpec((None, Q_STEP, AQ_W), lambda i, j: (i, j, 0)),
                  pl.BlockSpec((None, t, AK_W), lambda i, j: (i, 0, 0)),
                  pl.BlockSpec((nsb, A_KV_LATENT, ksb), lambda i, j: (i, 0, 0)),
                  _const_spec(wuk.shape), _const_spec(wuvt.shape), _const_spec(bias_tiles.shape),
                  _const_spec(tri.shape)],
        out_specs=pl.BlockSpec((None, Q_STEP, HEAD_W), lambda i, j: (i, j, 0)),
        out_shape=jax.ShapeDtypeStruct((b, t, HEAD_W), F32),
        scratch_shapes=[pltpu.VMEM((t, Q_STEP), I32), pltpu.VMEM((t, A_HEADS * Q_STEP), F32),
                        pltpu.VMEM((2 * A_KV_LATENT, A_HEADS * Q_STEP), F32)],
        compiler_params=_params("parallel", "arbitrary"),
        name="dsa",
    )(aq, ak, ckt, wuk, wuvt, bias_tiles, tri)


def _gla_static_tables():
    c = CHUNK
    idx = np.arange(c)
    tri = (idx[None, :] <= idx[:, None]).astype(np.float32)
    blocks = [tri]
    for s in GLA_MXU_LEVELS:
        end_left = (idx // (2 * s)) * 2 * s + s - 1
        blocks.append(tri[end_left])
    lev = np.full((c, c), -1, np.int32)
    for i in range(c):
        for j in range(c):
            if i == j:
                lev[i, j] = len(GLA_LEVELS)
            elif i > j:
                s = 1 << int(math.floor(math.log2(i ^ j)))
                lev[i, j] = GLA_LEVELS.index(s)
    pool = np.kron(np.eye(4, dtype=np.float32), np.full((64, 64), 1.0 / 64, np.float32))
    return np.concatenate(blocks, axis=0), np.tile(lev, (1, 4)), pool


def _expm1(x):
    return jnp.tanh(0.5 * x) * (jnp.exp(x) + 1.0)


def _log_sigmoid(x):
    return -(jnp.maximum(-x, 0.0) + jnp.log1p(jnp.exp(-jnp.abs(x))))


def _gla_core(q, k, v, g, mall, lev, st):
    c = CHUNK
    lane_head = lax.broadcasted_iota(I32, (c, HEAD_W), 1) // 64
    rowc = lax.broadcasted_iota(I32, (c, HEAD_W), 0)
    bc = sum(_dot(mall, part) for part in _split3(g))
    b = bc[0:c]
    blast = b[c - 1:c, :]
    att = jnp.zeros((c, HEAD_W), F32)
    for li, s in enumerate(GLA_LEVELS):
        if s in GLA_MXU_LEVELS:
            at = GLA_MXU_LEVELS.index(s) + 1
            mid = bc[at * c:(at + 1) * c]
        else:
            mid = jnp.concatenate([jnp.broadcast_to(b[p + s - 1:p + s, :], (2 * s, HEAD_W))
                                   for p in range(0, c, 2 * s)], axis=0)
        right = (rowc & s) != 0
        qt = jnp.where(right, q * jnp.exp(jnp.minimum(b - mid, 0.0)), 0.0)
        kt = jnp.where(right, 0.0, k * jnp.exp(jnp.minimum(mid - b, 0.0)))
        al = _dot_nt(qt.astype(BF16), _stack_heads(kt, lane_head).astype(BF16))
        att = jnp.where(lev == li, al, att)
    ad = _dot_nt(q.astype(BF16), _stack_heads(k, lane_head).astype(BF16))
    att = jnp.where(lev == len(GLA_LEVELS), ad, att)
    o = _dot(att.astype(BF16), _stack_heads(v, lane_head).astype(BF16))
    o = o + _dot_nt((q * jnp.exp(b)).astype(BF16), st.astype(BF16))
    ktail = k * jnp.exp(blast - b)
    upd = _dot(v.T.astype(BF16), ktail.astype(BF16))
    r2 = lax.broadcasted_iota(I32, (HEAD_W, HEAD_W), 0) // 64
    c2 = lax.broadcasted_iota(I32, (HEAD_W, HEAD_W), 1) // 64
    st = st * jnp.exp(blast) + jnp.where(r2 == c2, upd, 0.0)
    return o, st


def _gla_finish(o, r, ng, pool):
    ms = sum(_dot(part, pool) for part in _split3(o * o))
    return o * lax.rsqrt(ms + EPS) * ng * jax.nn.silu(r)


def _gla_b_kernel(p_ref, wgk_ref, bgk_ref, ng_ref, mall_ref, lev_ref, pool_ref, out_ref, st_ref):
    @pl.when(pl.program_id(1) == 0)
    def _():
        st_ref[...] = jnp.zeros_like(st_ref)

    tile = p_ref[...]
    q = tile[:, 0:256] * (B_DK ** -0.5)
    k = tile[:, 256:512]
    v = tile[:, 512:768]
    r = tile[:, 768:1024]
    z = _dot(tile[:, 1024:1152].astype(BF16), wgk_ref[...]) + bgk_ref[...]
    g = _log_sigmoid(z) / B_GATE_TAU
    mall, lev, st = mall_ref[...], lev_ref[...], st_ref[...]
    outs = []
    for c in range(tile.shape[0] // CHUNK):
        sl = slice(c * CHUNK, (c + 1) * CHUNK)
        o, st = _gla_core(q[sl], k[sl], v[sl], g[sl], mall, lev, st)
        outs.append(o)
    st_ref[...] = st
    out_ref[...] = _gla_finish(jnp.concatenate(outs, axis=0), r, ng_ref[...], pool_ref[...])


def _gla_d_kernel(p_ref, lbp_ref, ng_ref, mall_ref, lev_ref, pool_ref, out_ref, st_ref, *, layer):
    @pl.when(pl.program_id(1) == 0)
    def _():
        st_ref[...] = jnp.zeros_like(st_ref)

    lbp = lbp_ref[...]
    e = jnp.exp(lbp - jnp.max(lbp, axis=0, keepdims=True))
    soft = e / jnp.sum(e, axis=0, keepdims=True)
    cum = soft[0:1]
    for i in range(1, layer + 1):
        cum = cum + soft[i:i + 1]
    lb = cum - soft[0:1]

    tile = p_ref[...]
    q = jax.nn.silu(tile[:, 0:256])
    v = tile[:, 512:768]
    r = tile[:, 768:1024]
    t1 = jnp.log(jnp.maximum(lb, LB_FLOOR))
    t2 = jnp.log1p(-lb) + _log_sigmoid(tile[:, 256:512])
    g = jnp.maximum(t1, t2) + jnp.log1p(jnp.exp(-jnp.abs(t1 - t2)))
    k = -_expm1(g)
    mall, lev, st = mall_ref[...], lev_ref[...], st_ref[...]
    outs = []
    for c in range(tile.shape[0] // CHUNK):
        sl = slice(c * CHUNK, (c + 1) * CHUNK)
        o, st = _gla_core(q[sl], k[sl], v[sl], g[sl], mall, lev, st)
        outs.append(o)
    st_ref[...] = st
    out_ref[...] = _gla_finish(jnp.concatenate(outs, axis=0), r, ng_ref[...], pool_ref[...])


def _gla_call(kernel, p, extra, name):
    b, t, w = p.shape
    tt = min(TT_GLA, t)
    mall, lev, pool = _gla_static_tables()
    consts = list(extra) + [jnp.asarray(mall, BF16), jnp.asarray(lev), jnp.asarray(pool, BF16)]
    return pl.pallas_call(
        kernel,
        grid=(b, t // tt),
        in_specs=[pl.BlockSpec((None, tt, w), lambda i, j: (i, j, 0))]
        + [_const_spec(c.shape) for c in consts],
        out_specs=pl.BlockSpec((None, tt, HEAD_W), lambda i, j: (i, j, 0)),
        out_shape=jax.ShapeDtypeStruct((b, t, HEAD_W), F32),
        scratch_shapes=[pltpu.VMEM((HEAD_W, HEAD_W), F32)],
        compiler_params=_params("parallel", "arbitrary"),
        name=name,
    )(p, *consts)


def _lru_kernel(p_ref, cw_ref, cb_ref, wa_ref, ba_ref, wx_ref, bx_ref, lam_ref, out_ref,
                xpad_ref, h_ref):
    tt = p_ref.shape[0]
    pad = 8

    @pl.when(pl.program_id(1) == 0)
    def _():
        xpad_ref[0:pad, :] = jnp.zeros((pad, C_WIDTH), F32)
        h_ref[...] = jnp.zeros_like(h_ref)

    x = p_ref[:, 0:C_WIDTH]
    xpad_ref[pad:pad + tt, :] = x
    base = pad - (C_CONV - 1)
    xc = xpad_ref[base:base + tt, :] * cw_ref[0:1, :]
    for j in range(1, C_CONV):
        xc = xc + xpad_ref[base + j:base + j + tt, :] * cw_ref[j:j + 1, :]
    xc = xc + cb_ref[...]
    xpad_ref[0:pad, :] = x[tt - pad:tt, :]

    xb = xc.astype(BF16)
    r = jax.nn.sigmoid(_dot(xb, wa_ref[...]) + ba_ref[...])
    i = jax.nn.sigmoid(_dot(xb, wx_ref[...]) + bx_ref[...])
    lam = lam_ref[...]
    softplus_neg_lam = jnp.maximum(-lam, 0.0) + jnp.log1p(jnp.exp(-jnp.abs(lam)))
    log_a = -C_EXP * r * softplus_neg_lam
    a = jnp.exp(log_a)
    u = jnp.sqrt(jnp.maximum(-_expm1(2.0 * log_a), 0.0)) * (i * xc)

    rows = lax.broadcasted_iota(I32, (tt, C_WIDTH), 0)
    s = 1
    while s < tt:
        keep = rows >= s
        u = jnp.where(keep, a * pltpu.roll(u, s, 0) + u, u)
        a = jnp.where(keep, a * pltpu.roll(a, s, 0), a)
        s *= 2
    h = u + a * h_ref[0:1, :]
    h_ref[...] = jnp.broadcast_to(h[tt - 1:tt, :], h_ref.shape)
    out_ref[...] = h * jax.nn.gelu(p_ref[:, C_WIDTH:2 * C_WIDTH])


def _lru(pc, cw, cb, wa, ba, wx, bx, lam):
    b, t, w = pc.shape
    tt = min(TT_LRU, t)
    consts = [cw, cb, wa, ba, wx, bx, lam]
    return pl.pallas_call(
        _lru_kernel,
        grid=(b, t // tt),
        in_specs=[pl.BlockSpec((None, tt, w), lambda i, j: (i, j, 0))]
        + [_const_spec(c.shape) for c in consts],
        out_specs=pl.BlockSpec((None, tt, C_WIDTH), lambda i, j: (i, j, 0)),
        out_shape=jax.ShapeDtypeStruct((b, t, C_WIDTH), F32),
        scratch_shapes=[pltpu.VMEM((tt + 8, C_WIDTH), F32), pltpu.VMEM((8, C_WIDTH), F32)],
        compiler_params=_params("parallel", "arbitrary"),
        name="rglru",
    )(pc, *consts)


def _merge_kernel(x_ref, ya_ref, yb_ref, yc_ref, yd_ref, g_ref, wg_ref, bg_ref, wbr_ref, wo_ref,
                  out_ref):
    x = x_ref[...]
    d = x.shape[1]
    h = _rms(x, g_ref[...]).astype(BF16)
    merged = jnp.zeros(x.shape, F32)
    for n, y_ref in enumerate((ya_ref, yb_ref, yc_ref, yd_ref)):
        gate = jax.nn.sigmoid(_dot(h, wg_ref[:, n * d:(n + 1) * d]) + bg_ref[:, n * d:(n + 1) * d])
        merged = merged + gate * _dot(y_ref[...].astype(BF16), wbr_ref[n])
    out_ref[...] = x + _dot(merged.astype(BF16), wo_ref[...])


def _merge(x2, ys, g, wg, bg, wbr, wo):
    n, d = x2.shape
    tm = min(TM_PROJ, n)
    consts = [g, wg, bg, wbr, wo]
    return pl.pallas_call(
        _merge_kernel,
        grid=(n // tm,),
        in_specs=[pl.BlockSpec((tm, d), lambda i: (i, 0))]
        + [pl.BlockSpec((tm, HEAD_W), lambda i: (i, 0)) for _ in ys]
        + [_const_spec(c.shape) for c in consts],
        out_specs=pl.BlockSpec((tm, d), lambda i: (i, 0)),
        out_shape=jax.ShapeDtypeStruct((n, d), F32),
        compiler_params=_params("parallel"),
        name="merge",
    )(x2, *ys, *consts)


def _ffn_kernel(x_ref, g_ref, w1_ref, w2_ref, gf_ref, out_ref, *, final):
    x = x_ref[...]
    d = x.shape[1]
    h = _rms(x, g_ref[...]).astype(BF16)
    acc = x
    for c in range(w1_ref.shape[1] // d):
        hid = jnp.square(jnp.maximum(_dot(h, w1_ref[:, c * d:(c + 1) * d]), 0.0))
        acc = acc + _dot(hid.astype(BF16), w2_ref[c * d:(c + 1) * d, :])
    if final:
        acc = _rms(acc, gf_ref[...])
    out_ref[...] = acc


def _ffn(x2, g, w1, w2, gf, final):
    n, d = x2.shape
    tm = min(TM_PROJ, n)
    consts = [g, w1, w2, gf]
    return pl.pallas_call(
        functools.partial(_ffn_kernel, final=final),
        grid=(n // tm,),
        in_specs=[pl.BlockSpec((tm, d), lambda i: (i, 0))] + [_const_spec(c.shape) for c in consts],
        out_specs=pl.BlockSpec((tm, d), lambda i: (i, 0)),
        out_shape=jax.ShapeDtypeStruct((n, d), F32),
        compiler_params=_params("parallel"),
        name="ffn",
    )(x2, *consts)


def _w_in_relayout_kernel(w_ref, out_ref):
    w = w_ref[...]
    offs = np.concatenate([[0], np.cumsum(IN_SPLITS)])
    (a_q, a_ckv, a_iq, a_ik, a_iw, b_q, b_k, b_v, b_lr, b_r, c_x, c_y, d_q, d_f, d_i, d_g) = (
        w[:, offs[i]:offs[i + 1]] for i in range(len(IN_SPLITS)))
    pad = None
    parts = [a_q, a_iq, a_iw, pad,
             a_ckv, a_ik, a_ik, a_ik, a_ik,
             b_q, b_k, b_v, b_r, b_lr, pad,
             c_x, c_y, d_q, d_f, d_i, d_g]
    out_ref[...] = jnp.zeros(out_ref.shape, out_ref.dtype)
    o = 0
    for part in parts:
        if part is None:
            o = -(-o // LANES) * LANES
            continue
        out_ref[:, o:o + part.shape[1]] = part.astype(BF16)
        o += part.shape[1]
    assert o == IN_W


def _w_in_relayout(w):
    d, cols = w.shape
    rows = min(128, d)
    return pl.pallas_call(
        _w_in_relayout_kernel,
        grid=(d // rows,),
        in_specs=[pl.BlockSpec((rows, cols), lambda i: (i, 0))],
        out_specs=pl.BlockSpec((rows, IN_W), lambda i: (i, 0)),
        out_shape=jax.ShapeDtypeStruct((d, IN_W), BF16),
        compiler_params=_params("parallel"),
        name="w_in_relayout",
    )(w)


def _block_diag(w):
    n, bi, bj = w.shape
    eye = jnp.eye(n, dtype=w.dtype)
    return (w[:, :, None, :] * eye[:, None, :, None]).reshape(n * bi, n * bj)


def _wide_heads(w):
    h, c, d = w.shape
    eye = jnp.eye(h, dtype=w.dtype)
    return (w[:, :, None, :] * eye[:, None, :, None]).reshape(h, c, h * d)


def kernel(x, norm1_g, w_in, w_gate, b_gate, kv_norm_g, w_uk, w_uv, rel_bias, w_gk2, b_gk, gla_norm_g, conv_w, conv_b, w_rg_a, b_rg_a, w_rg_x, b_rg_x, lru_lambda, lb_param, hgrn_norm_g, w_br_a, w_br_b, w_br_c, w_br_d, w_out, norm2_g, w_ff1, w_ff2, final_norm_g):
    bsz, t, d = x.shape
    depth = w_in.shape[0]
    n = bsz * t
    row = lambda v: v.reshape(1, -1)
    bias_tiles = _bias_tiles(rel_bias)
    x2 = x.reshape(n, d)
    for l in range(depth):
        aq, ak, ckt, pb, pc, pd = _inproj(x2, row(norm1_g[l]), _w_in_relayout(w_in[l]), row(kv_norm_g[l]))
        to3 = lambda a: a.reshape(bsz, t, a.shape[-1])

        y_a = _dsa(to3(aq), to3(ak), ckt, w_uk[l].reshape(A_HEADS * A_HEAD_DIM, A_KV_LATENT).astype(BF16),
                   jnp.swapaxes(_wide_heads(w_uv[l]), 1, 2).astype(BF16), bias_tiles)

        wgk = jnp.concatenate([w_gk2[l], jnp.zeros((LANES - B_GATE_RANK, HEAD_W), F32)], axis=0)
        y_b = _gla_call(_gla_b_kernel, to3(pb),
                        [wgk.astype(BF16), row(b_gk[l]), row(jnp.tile(gla_norm_g[l], B_HEADS))], "gla")

        y_c = _lru(to3(pc), conv_w[l], row(conv_b[l]), _block_diag(w_rg_a[l]).astype(BF16),
                   row(b_rg_a[l]), _block_diag(w_rg_x[l]).astype(BF16), row(b_rg_x[l]),
                   row(lru_lambda[l]))

        y_d = _gla_call(functools.partial(_gla_d_kernel, layer=l), to3(pd),
                        [lb_param, row(jnp.tile(hgrn_norm_g[l], D_HEADS))], "hgrn2")

        ys = [y.reshape(n, HEAD_W) for y in (y_a, y_b, y_c, y_d)]
        wbr = jnp.stack([w_br_a[l], w_br_b[l], w_br_c[l], w_br_d[l]]).astype(BF16)
        x2 = _merge(x2, ys, row(norm1_g[l]), w_gate[l].astype(BF16), row(b_gate[l]), wbr,
                    w_out[l].astype(BF16))
        x2 = _ffn(x2, row(norm2_g[l]), w_ff1[l].astype(BF16), w_ff2[l].astype(BF16),
                  row(final_norm_g), final=(l == depth - 1))
    return x2.reshape(bsz, t, d)
```

```python
import functools
import math

import numpy as np
import jax
import jax.numpy as jnp
from jax import lax
from jax.experimental import pallas as pl
from jax.experimental.pallas import tpu as pltpu

F32 = jnp.float32
BF16 = jnp.bfloat16
I32 = jnp.int32

EPS = 1e-6
NEG_BIG = -1e30
UNSELECTED = -3e38
LB_FLOOR = 1e-20
HI16_MASK = -(2 ** 16)
BF16_MIN_NORMAL_BITS = 0x0080
LOG2E = 1.4426950408889634

A_HEADS = 4
A_HEAD_DIM = 64
A_KV_LATENT = 128
IDX_HEADS = 4
IDX_DIM = 64
MAX_TOPK = 256
Q_BLOCK = 128
Q_STEP = 256
SOFTMAX_DENOM_FLOOR = 2.0 ** -40
N_BUCKETS = 32
MAX_DISTANCE = 128
B_HEADS = 4
B_DK = 64
B_DV = 64
B_GATE_RANK = 16
B_GATE_TAU = 16.0
C_WIDTH = 256
C_BLOCKS = 4
C_CONV = 4
C_EXP = 8.0
D_HEADS = 4
D_DK = 64
D_DV = 64
CHUNK = 64
N_BRANCH = 4

HEAD_W = 256
LANES = 128

IN_SPLITS = (
    A_HEADS * A_HEAD_DIM, A_KV_LATENT, IDX_HEADS * IDX_DIM, IDX_DIM, IDX_HEADS,
    B_HEADS * B_DK, B_HEADS * B_DK, B_HEADS * B_DV, B_GATE_RANK, B_HEADS * B_DV,
    C_WIDTH, C_WIDTH,
    D_HEADS * D_DK, D_HEADS * D_DK, D_HEADS * D_DV, D_HEADS * D_DV,
)

AQ_W = 640
AK_W = 384
PB_W = 1152
PC_W = 512
PD_W = 1024
IN_W = AQ_W + AK_W + PB_W + PC_W + PD_W

VMEM_LIMIT = 56 * 1024 * 1024

TM_PROJ = 512
TT_GLA = 256
TT_LRU = 256
GLA_LEVELS = (32, 16, 8, 4, 2, 1)
GLA_MXU_LEVELS = (2, 1)


def _rms(x, g):
    return x * lax.rsqrt(jnp.mean(x * x, axis=-1, keepdims=True) + EPS) * g


def _dot(a, b):
    return jnp.dot(a, b, preferred_element_type=F32)


def _dot_nt(a, b):
    return lax.dot_general(a, b, (((1,), (1,)), ((), ())), preferred_element_type=F32)


def _split3(x):
    hi = x.astype(BF16)
    rest = x - hi.astype(F32)
    mid = rest.astype(BF16)
    lo = (rest - mid.astype(F32)).astype(BF16)
    return lo, mid, hi


def _stack_heads(a, lane_head):
    return jnp.concatenate([jnp.where(lane_head == h, a, 0.0) for h in range(4)], axis=0)


def _params(*sem):
    return pltpu.CompilerParams(dimension_semantics=sem, vmem_limit_bytes=VMEM_LIMIT)


def _const_spec(shape):
    nd = len(shape)
    return pl.BlockSpec(shape, lambda *_: (0,) * nd)


def _inproj_kernel(x_ref, g_ref, w_ref, kvg_ref, aq_ref, ak_ref, ckt_ref, pb_ref, pc_ref, pd_ref):
    h = _rms(x_ref[...], g_ref[...]).astype(BF16)
    o = 0
    aq_ref[...] = _dot(h, w_ref[:, o:o + AQ_W])
    o += AQ_W
    ak = _dot(h, w_ref[:, o:o + AK_W])
    ckv = _rms(ak[:, :A_KV_LATENT], kvg_ref[...])
    ak_ref[:, :A_KV_LATENT] = ckv.astype(BF16)
    ak_ref[:, A_KV_LATENT:] = ak[:, A_KV_LATENT:].astype(BF16)
    ckt_ref[0] = ckv.T.astype(BF16)
    o += AK_W
    pb_ref[...] = _dot(h, w_ref[:, o:o + PB_W])
    o += PB_W
    pc_ref[...] = _dot(h, w_ref[:, o:o + PC_W])
    o += PC_W
    pd_ref[...] = _dot(h, w_ref[:, o:o + PD_W])


def _inproj(x2, g, w, kvg):
    n, d = x2.shape
    tm = min(TM_PROJ, n)
    def rows(wd, dt):
        return pl.BlockSpec((tm, wd), lambda i: (i, 0)), jax.ShapeDtypeStruct((n, wd), dt)

    ckt = (pl.BlockSpec((1, A_KV_LATENT, tm), lambda i: (i, 0, 0)),
           jax.ShapeDtypeStruct((n // tm, A_KV_LATENT, tm), BF16))
    outs = [rows(AQ_W, F32), rows(AK_W, BF16), ckt, rows(PB_W, F32), rows(PC_W, F32), rows(PD_W, F32)]
    return pl.pallas_call(
        _inproj_kernel,
        grid=(n // tm,),
        in_specs=[pl.BlockSpec((tm, d), lambda i: (i, 0)), _const_spec((1, d)),
                  _const_spec((d, IN_W)), _const_spec((1, A_KV_LATENT))],
        out_specs=[o[0] for o in outs],
        out_shape=[o[1] for o in outs],
        compiler_params=_params("parallel"),
        name="inproj",
    )(x2, g, w, kvg)


def _t5_bucket(dist):
    n = jnp.maximum(dist, 0)
    max_exact = N_BUCKETS // 2
    nf = jnp.maximum(n, max_exact).astype(F32)
    large = max_exact + (jnp.log(nf / max_exact) / math.log(MAX_DISTANCE / max_exact)
                         * (N_BUCKETS - max_exact)).astype(I32)
    large = jnp.minimum(large, N_BUCKETS - 1)
    return jnp.where(n < max_exact, n, large)


def _bias_tiles_kernel(rb_ref, out_ref):
    krow = lax.broadcasted_iota(I32, (Q_BLOCK, Q_BLOCK), 0)
    qcol = lax.broadcasted_iota(I32, (Q_BLOCK, Q_BLOCK), 1)
    for d in range(3):
        bucket = _t5_bucket(d * Q_BLOCK + qcol - krow)
        for h in range(A_HEADS):
            acc = jnp.zeros((Q_BLOCK, Q_BLOCK), F32)
            for k in range(N_BUCKETS):
                acc = jnp.where(bucket == k, rb_ref[k, h], acc)
            out_ref[d, :, h * Q_BLOCK:(h + 1) * Q_BLOCK] = acc


def _bias_tiles(rel_bias):
    return pl.pallas_call(
        _bias_tiles_kernel,
        in_specs=[pl.BlockSpec(memory_space=pltpu.SMEM)],
        out_specs=pl.BlockSpec(memory_space=pltpu.VMEM),
        out_shape=jax.ShapeDtypeStruct((3, Q_BLOCK, A_HEADS * Q_BLOCK), F32),
        name="bias_tiles",
    )(rel_bias)


def _dsa_kernel(aq_ref, ak_ref, ckt_ref, wuk_ref, wuvt_ref, bias_ref, tri_ref, out_ref,
                keys_ref, khi_ref, s_ref, acc_ref, *, topk, ksb, nu):
    kg = Q_BLOCK
    qw = nu * kg
    ng = ksb // kg
    qi = pl.program_id(1)
    nsb = ((qi + 1) * nu - 1) // ng + 1
    aq = aq_ref[...]
    lane_head = lax.broadcasted_iota(I32, (qw, HEAD_W), 1) // A_HEAD_DIM
    krow = lax.broadcasted_iota(I32, (kg, kg), 0)
    qcol = lax.broadcasted_iota(I32, (kg, kg), 1)

    def causal(sb, g, u):
        return sb * ksb + g * kg + krow <= (qi * nu + u) * kg + qcol

    groups = [(g, slice(g * kg, (g + 1) * kg)) for g in range(ng)]
    heads = [(h, slice(h * qw, (h + 1) * qw)) for h in range(A_HEADS)]
    blocks_u = [(u, slice(u * kg, (u + 1) * kg)) for u in range(nu)]

    def lanes(h, u):
        return slice(h * qw + u * kg, h * qw + (u + 1) * kg)

    def fold_rows(a, op, rows):
        return op(a.reshape(a.shape[0] // rows, rows, a.shape[1]), axis=0)

    qx = _stack_heads(aq[:, 256:512], lane_head).astype(BF16)
    qlat = _dot(_stack_heads(aq[:, 0:256], lane_head).astype(BF16), wuk_ref[...])
    qlat = (qlat * (A_HEAD_DIM ** -0.5)).astype(BF16)
    iw_t = (aq[:, 512:640] * (IDX_HEADS ** -0.5 * IDX_DIM ** -0.5)).T
    iw_rows = [iw_t[h:h + 1, :] for h in range(IDX_HEADS)]

    def score_body(sb, top):
        off = pl.multiple_of(sb * ksb, ksb)
        ik = ak_ref[pl.ds(off, ksb), A_KV_LATENT:]
        ck = ak_ref[pl.ds(off, ksb), :A_KV_LATENT]
        rel = jnp.maximum(_dot_nt(ik, qx), 0.0)
        st = _dot_nt(ck, qlat)
        for g, rs in groups:
            rows = pl.ds(off + g * kg, kg)
            sc = jnp.zeros((kg, qw), F32)
            for h, cs in heads:
                sc = sc + rel[rs, cs] * iw_rows[h]
            peak = {}
            sc = jnp.where(sc == 0.0, 0.0, sc)
            for u, us in blocks_u:
                bits = pltpu.bitcast(jnp.where(causal(sb, g, u), sc[:, us], NEG_BIG), I32)
                keys_ref[rows, us] = bits ^ ((bits >> 31) & 0x7FFFFFFF)
                khi_ref[rows, us] = pltpu.bitcast(bits & HI16_MASK, F32).astype(BF16)
                bias = bias_ref[jnp.clip(qi * nu + u - (sb * ng + g), 0, 2)]
                for h, _ in heads:
                    logit = (st[rs, lanes(h, u)] + bias[:, h * kg:(h + 1) * kg]) * LOG2E
                    s_ref[rows, lanes(h, u)] = logit
                    peak[h, u] = fold_rows(logit, jnp.max, 8)
            top = jnp.maximum(top, jnp.concatenate(
                [peak[h, u] for h, _ in heads for u, _ in blocks_u], axis=1))
        return top

    top = lax.fori_loop(0, nsb, score_body, jnp.full((8, A_HEADS * qw), NEG_BIG, F32))
    m_bound = jnp.max(top, axis=0, keepdims=True)

    acc_rows = 32

    def count(pred_fn):
        def body(sb, acc):
            off = pl.multiple_of(sb * ksb, ksb)
            hit = jnp.where(pred_fn(keys_ref[pl.ds(off, ksb), :]), 1.0, 0.0)
            return acc + fold_rows(hit, jnp.sum, acc_rows)
        acc = lax.fori_loop(0, nsb, body, jnp.zeros((acc_rows, qw), F32))
        return jnp.sum(acc, axis=0, keepdims=True)

    def count_hi(cand_bf):
        def body(sb, acc):
            off = pl.multiple_of(sb * ksb, ksb)
            hit = jnp.where(khi_ref[pl.ds(off, ksb), :] >= cand_bf,
                            jnp.ones((), BF16), jnp.zeros((), BF16))
            parts = [hit[r:r + acc_rows] for r in range(0, ksb, acc_rows)]
            while len(parts) > 1:
                parts = [a + b for a, b in zip(parts[::2], parts[1::2])]
            return acc + parts[0]
        acc = lax.fori_loop(0, nsb, body, jnp.zeros((acc_rows, qw), BF16))
        return jnp.sum(acc.astype(F32), axis=0, keepdims=True)

    def hi_bit_body(i, carry):
        theta, cnt_ge = carry
        cand = theta + jnp.left_shift(jnp.int32(1), 15 - i)
        raw = cand ^ ((cand >> 15) & 0x7FFF)
        raw = jnp.where((raw > 0) & (raw < BF16_MIN_NORMAL_BITS), BF16_MIN_NORMAL_BITS, raw)
        cand_bf = pltpu.bitcast(jnp.left_shift(raw, 16), F32).astype(BF16)
        cnt = count_hi(cand_bf)
        ok = cnt >= topk
        return jnp.where(ok, cand, theta), jnp.where(ok, cnt, cnt_ge)

    def bit_body(i, carry):
        theta, cnt_ge = carry
        cand = theta + jnp.left_shift(jnp.int32(1), 15 - i)
        cnt = count(lambda key: key >= cand)
        ok = cnt >= topk
        return jnp.where(ok, cand, theta), jnp.where(ok, cnt, cnt_ge)

    init = (jnp.full((1, qw), -(2 ** 15), I32), jnp.zeros((1, qw), F32) + (nsb * ksb).astype(F32))
    theta_hi, cnt_ge = lax.fori_loop(0, 16, hi_bit_body, init)
    theta, cnt_ge = lax.fori_loop(0, 16, bit_body, (theta_hi * 65536, cnt_ge))
    need = topk - count(lambda key: key > theta)
    tie_break = jnp.max(cnt_ge) > topk

    ones = jnp.ones((A_KV_LATENT, ksb), BF16)

    def masked_logits(ranked, sb, tie_seen):
        off = pl.multiple_of(sb * ksb, ksb)
        key = keys_ref[pl.ds(off, ksb), :]
        if ranked:
            eqf = jnp.where(key == theta, 1.0, 0.0)
            rank = tie_seen + _dot(tri_ref[...], eqf.astype(BF16))
            tie_seen = rank[ksb - 1:ksb] + eqf[ksb - 1:ksb]

        def selected(rs, us):
            k, th = key[rs, us], theta[:, us]
            if ranked:
                return (k > th) | ((k == th) & (rank[rs, us] < need[:, us]))
            return k >= th

        out = []
        for g, rs in groups:
            raw = s_ref[pl.ds(off + g * kg, kg), :]
            valid = [selected(rs, us) & causal(sb, g, u) for u, us in blocks_u]
            out.append(jnp.concatenate(
                [jnp.where(valid[u], raw[:, lanes(h, u)], UNSELECTED)
                 for h, _ in heads for u, _ in blocks_u], axis=1))
        return jnp.concatenate(out, axis=0), tie_seen

    def accumulate(sb, p, scale=None):
        new = _dot(jnp.concatenate([ckt_ref[sb], ones], axis=0), p.astype(BF16))
        acc_ref[...] = new + (acc_ref[...] if scale is None else acc_ref[...] * scale)

    def attend_online(ranked):
        acc_ref[...] = jnp.zeros_like(acc_ref)

        def body(sb, carry):
            m, tie_seen = carry
            logits, tie_seen = masked_logits(ranked, sb, tie_seen)
            m_new = jnp.maximum(m, jnp.max(fold_rows(logits, jnp.max, 8), axis=0, keepdims=True))
            accumulate(sb, jnp.exp2(logits - m_new), jnp.exp2(m - m_new))
            return m_new, tie_seen

        lax.fori_loop(0, nsb, body,
                      (jnp.full((1, A_HEADS * qw), NEG_BIG, F32), jnp.zeros((1, qw), F32)))

    def attend_bounded():
        acc_ref[...] = jnp.zeros_like(acc_ref)

        def body(sb, _):
            logits, _ = masked_logits(False, sb, None)
            accumulate(sb, jnp.exp2(logits - m_bound))
            return 0

        lax.fori_loop(0, nsb, body, 0)
        smallest = jnp.min(acc_ref[A_KV_LATENT:A_KV_LATENT + 1, :])

        @pl.when(jnp.logical_not(smallest > SOFTMAX_DENOM_FLOOR))
        def _():
            attend_online(False)

    lax.cond(tie_break, lambda: attend_online(True), attend_bounded)
    acc = acc_ref[...]
    o_t = (acc[:A_KV_LATENT] / acc[A_KV_LATENT:]).astype(BF16)
    y_t = jnp.zeros((HEAD_W, qw), F32)
    for h, cs in heads:
        y_t = y_t + _dot(wuvt_ref[h], o_t[:, cs])
    out_ref[...] = y_t.T


def _dsa(aq, ak, ckt, wuk, wuvt, bias_tiles):
    b, t, _ = aq.shape
    ksb = ckt.shape[2]
    nu = Q_STEP // Q_BLOCK
    assert t % ksb == 0 and ksb % Q_BLOCK == 0 and ckt.shape[0] * ksb == b * t and t % Q_STEP == 0
    nsb = t // ksb
    topk = min(MAX_TOPK, t // 4)
    idx = np.arange(ksb)
    tri = jnp.asarray(idx[None, :] < idx[:, None], BF16)
    return pl.pallas_call(
        functools.partial(_dsa_kernel, topk=topk, ksb=ksb, nu=nu),
        grid=(b, t // Q_STEP),
        in_specs=[pl.BlockSpec((None, Q_STEP, AQ_W), lambda i, j: (i, j, 0)),
                  pl.BlockSpec((None, t, AK_W), lambda i, j: (i, 0, 0)),
                  pl.BlockSpec((nsb, A_KV_LATENT, ksb), lambda i, j: (i, 0, 0)),
                  _const_spec(wuk.shape), _const_spec(wuvt.shape), _const_spec(bias_tiles.shape),
                  _const_spec(tri.shape)],
        out_specs=pl.BlockSpec((None, Q_STEP, HEAD_W), lambda i, j: (i, j, 0)),
        out_shape=jax.ShapeDtypeStruct((b, t, HEAD_W), F32),
        scratch_shapes=[pltpu.VMEM((t, Q_STEP), I32), pltpu.VMEM((t, Q_STEP), BF16),
                        pltpu.VMEM((t, A_HEADS * Q_STEP), F32),
                        pltpu.VMEM((2 * A_KV_LATENT, A_HEADS * Q_STEP), F32)],
        compiler_params=_params("parallel", "arbitrary"),
        name="dsa",
    )(aq, ak, ckt, wuk, wuvt, bias_tiles, tri)


def _gla_static_tables():
    c = CHUNK
    idx = np.arange(c)
    tri = (idx[None, :] <= idx[:, None]).astype(np.float32)
    blocks = [tri]
    for s in GLA_MXU_LEVELS:
        end_left = (idx // (2 * s)) * 2 * s + s - 1
        blocks.append(tri[end_left])
    lev = np.full((c, c), -1, np.int32)
    for i in range(c):
        for j in range(c):
            if i == j:
                lev[i, j] = len(GLA_LEVELS)
            elif i > j:
                s = 1 << int(math.floor(math.log2(i ^ j)))
                lev[i, j] = GLA_LEVELS.index(s)
    pool = np.kron(np.eye(4, dtype=np.float32), np.full((64, 64), 1.0 / 64, np.float32))
    return np.concatenate(blocks, axis=0), np.tile(lev, (1, 4)), pool


def _expm1(x):
    return jnp.tanh(0.5 * x) * (jnp.exp(x) + 1.0)


def _log_sigmoid(x):
    return -(jnp.maximum(-x, 0.0) + jnp.log1p(jnp.exp(-jnp.abs(x))))


def _gla_core(q, k, v, g, mall, lev, st):
    c = CHUNK
    lane_head = lax.broadcasted_iota(I32, (c, HEAD_W), 1) // 64
    rowc = lax.broadcasted_iota(I32, (c, HEAD_W), 0)
    bc = sum(_dot(mall, part) for part in _split3(g))
    b = bc[0:c]
    blast = b[c - 1:c, :]
    att = jnp.zeros((c, HEAD_W), F32)
    for li, s in enumerate(GLA_LEVELS):
        if s in GLA_MXU_LEVELS:
            at = GLA_MXU_LEVELS.index(s) + 1
            mid = bc[at * c:(at + 1) * c]
        else:
            mid = jnp.concatenate([jnp.broadcast_to(b[p + s - 1:p + s, :], (2 * s, HEAD_W))
                                   for p in range(0, c, 2 * s)], axis=0)
        right = (rowc & s) != 0
        qt = jnp.where(right, q * jnp.exp(jnp.minimum(b - mid, 0.0)), 0.0)
        kt = jnp.where(right, 0.0, k * jnp.exp(jnp.minimum(mid - b, 0.0)))
        al = _dot_nt(qt.astype(BF16), _stack_heads(kt, lane_head).astype(BF16))
        att = jnp.where(lev == li, al, att)
    ad = _dot_nt(q.astype(BF16), _stack_heads(k, lane_head).astype(BF16))
    att = jnp.where(lev == len(GLA_LEVELS), ad, att)
    o = _dot(att.astype(BF16), _stack_heads(v, lane_head).astype(BF16))
    o = o + _dot_nt((q * jnp.exp(b)).astype(BF16), st.astype(BF16))
    ktail = k * jnp.exp(blast - b)
    upd = _dot(v.T.astype(BF16), ktail.astype(BF16))
    r2 = lax.broadcasted_iota(I32, (HEAD_W, HEAD_W), 0) // 64
    c2 = lax.broadcasted_iota(I32, (HEAD_W, HEAD_W), 1) // 64
    st = st * jnp.exp(blast) + jnp.where(r2 == c2, upd, 0.0)
    return o, st


def _gla_finish(o, r, ng, pool):
    ms = sum(_dot(part, pool) for part in _split3(o * o))
    return o * lax.rsqrt(ms + EPS) * ng * jax.nn.silu(r)


def _gla_b_kernel(p_ref, wgk_ref, bgk_ref, ng_ref, mall_ref, lev_ref, pool_ref, out_ref, st_ref):
    @pl.when(pl.program_id(1) == 0)
    def _():
        st_ref[...] = jnp.zeros_like(st_ref)

    tile = p_ref[...]
    q = tile[:, 0:256] * (B_DK ** -0.5)
    k = tile[:, 256:512]
    v = tile[:, 512:768]
    r = tile[:, 768:1024]
    z = _dot(tile[:, 1024:1152].astype(BF16), wgk_ref[...]) + bgk_ref[...]
    g = _log_sigmoid(z) / B_GATE_TAU
    mall, lev, st = mall_ref[...], lev_ref[...], st_ref[...]
    outs = []
    for c in range(tile.shape[0] // CHUNK):
        sl = slice(c * CHUNK, (c + 1) * CHUNK)
        o, st = _gla_core(q[sl], k[sl], v[sl], g[sl], mall, lev, st)
        outs.append(o)
    st_ref[...] = st
    out_ref[...] = _gla_finish(jnp.concatenate(outs, axis=0), r, ng_ref[...], pool_ref[...])


def _gla_d_kernel(p_ref, lbp_ref, ng_ref, mall_ref, lev_ref, pool_ref, out_ref, st_ref, *, layer):
    @pl.when(pl.program_id(1) == 0)
    def _():
        st_ref[...] = jnp.zeros_like(st_ref)

    lbp = lbp_ref[...]
    e = jnp.exp(lbp - jnp.max(lbp, axis=0, keepdims=True))
    soft = e / jnp.sum(e, axis=0, keepdims=True)
    cum = soft[0:1]
    for i in range(1, layer + 1):
        cum = cum + soft[i:i + 1]
    lb = cum - soft[0:1]

    tile = p_ref[...]
    q = jax.nn.silu(tile[:, 0:256])
    v = tile[:, 512:768]
    r = tile[:, 768:1024]
    t1 = jnp.log(jnp.maximum(lb, LB_FLOOR))
    t2 = jnp.log1p(-lb) + _log_sigmoid(tile[:, 256:512])
    g = jnp.maximum(t1, t2) + jnp.log1p(jnp.exp(-jnp.abs(t1 - t2)))
    k = -_expm1(g)
    mall, lev, st = mall_ref[...], lev_ref[...], st_ref[...]
    outs = []
    for c in range(tile.shape[0] // CHUNK):
        sl = slice(c * CHUNK, (c + 1) * CHUNK)
        o, st = _gla_core(q[sl], k[sl], v[sl], g[sl], mall, lev, st)
        outs.append(o)
    st_ref[...] = st
    out_ref[...] = _gla_finish(jnp.concatenate(outs, axis=0), r, ng_ref[...], pool_ref[...])


def _gla_call(kernel, p, extra, name):
    b, t, w = p.shape
    tt = min(TT_GLA, t)
    mall, lev, pool = _gla_static_tables()
    consts = list(extra) + [jnp.asarray(mall, BF16), jnp.asarray(lev), jnp.asarray(pool, BF16)]
    return pl.pallas_call(
        kernel,
        grid=(b, t // tt),
        in_specs=[pl.BlockSpec((None, tt, w), lambda i, j: (i, j, 0))]
        + [_const_spec(c.shape) for c in consts],
        out_specs=pl.BlockSpec((None, tt, HEAD_W), lambda i, j: (i, j, 0)),
        out_shape=jax.ShapeDtypeStruct((b, t, HEAD_W), F32),
        scratch_shapes=[pltpu.VMEM((HEAD_W, HEAD_W), F32)],
        compiler_params=_params("parallel", "arbitrary"),
        name=name,
    )(p, *consts)


def _lru_kernel(p_ref, cw_ref, cb_ref, wa_ref, ba_ref, wx_ref, bx_ref, lam_ref, out_ref,
                xpad_ref, h_ref):
    tt = p_ref.shape[0]
    pad = 8

    @pl.when(pl.program_id(1) == 0)
    def _():
        xpad_ref[0:pad, :] = jnp.zeros((pad, C_WIDTH), F32)
        h_ref[...] = jnp.zeros_like(h_ref)

    x = p_ref[:, 0:C_WIDTH]
    xpad_ref[pad:pad + tt, :] = x
    base = pad - (C_CONV - 1)
    xc = xpad_ref[base:base + tt, :] * cw_ref[0:1, :]
    for j in range(1, C_CONV):
        xc = xc + xpad_ref[base + j:base + j + tt, :] * cw_ref[j:j + 1, :]
    xc = xc + cb_ref[...]
    xpad_ref[0:pad, :] = x[tt - pad:tt, :]

    xb = xc.astype(BF16)
    r = jax.nn.sigmoid(_dot(xb, wa_ref[...]) + ba_ref[...])
    i = jax.nn.sigmoid(_dot(xb, wx_ref[...]) + bx_ref[...])
    lam = lam_ref[...]
    softplus_neg_lam = jnp.maximum(-lam, 0.0) + jnp.log1p(jnp.exp(-jnp.abs(lam)))
    log_a = -C_EXP * r * softplus_neg_lam
    a = jnp.exp(log_a)
    u = jnp.sqrt(jnp.maximum(-_expm1(2.0 * log_a), 0.0)) * (i * xc)

    rows = lax.broadcasted_iota(I32, (tt, C_WIDTH), 0)
    s = 1
    while s < tt:
        keep = rows >= s
        u = jnp.where(keep, a * pltpu.roll(u, s, 0) + u, u)
        a = jnp.where(keep, a * pltpu.roll(a, s, 0), a)
        s *= 2
    h = u + a * h_ref[0:1, :]
    h_ref[...] = jnp.broadcast_to(h[tt - 1:tt, :], h_ref.shape)
    out_ref[...] = h * jax.nn.gelu(p_ref[:, C_WIDTH:2 * C_WIDTH])


def _lru(pc, cw, cb, wa, ba, wx, bx, lam):
    b, t, w = pc.shape
    tt = min(TT_LRU, t)
    consts = [cw, cb, wa, ba, wx, bx, lam]
    return pl.pallas_call(
        _lru_kernel,
        grid=(b, t // tt),
        in_specs=[pl.BlockSpec((None, tt, w), lambda i, j: (i, j, 0))]
        + [_const_spec(c.shape) for c in consts],
        out_specs=pl.BlockSpec((None, tt, C_WIDTH), lambda i, j: (i, j, 0)),
        out_shape=jax.ShapeDtypeStruct((b, t, C_WIDTH), F32),
        scratch_shapes=[pltpu.VMEM((tt + 8, C_WIDTH), F32), pltpu.VMEM((8, C_WIDTH), F32)],
        compiler_params=_params("parallel", "arbitrary"),
        name="rglru",
    )(pc, *consts)


def _merge_kernel(x_ref, ya_ref, yb_ref, yc_ref, yd_ref, g_ref, wg_ref, bg_ref, wbr_ref, wo_ref,
                  out_ref):
    x = x_ref[...]
    d = x.shape[1]
    h = _rms(x, g_ref[...]).astype(BF16)
    merged = jnp.zeros(x.shape, F32)
    for n, y_ref in enumerate((ya_ref, yb_ref, yc_ref, yd_ref)):
        gate = jax.nn.sigmoid(_dot(h, wg_ref[:, n * d:(n + 1) * d]) + bg_ref[:, n * d:(n + 1) * d])
        merged = merged + gate * _dot(y_ref[...].astype(BF16), wbr_ref[n])
    out_ref[...] = x + _dot(merged.astype(BF16), wo_ref[...])


def _merge(x2, ys, g, wg, bg, wbr, wo):
    n, d = x2.shape
    tm = min(TM_PROJ, n)
    consts = [g, wg, bg, wbr, wo]
    return pl.pallas_call(
        _merge_kernel,
        grid=(n // tm,),
        in_specs=[pl.BlockSpec((tm, d), lambda i: (i, 0))]
        + [pl.BlockSpec((tm, HEAD_W), lambda i: (i, 0)) for _ in ys]
        + [_const_spec(c.shape) for c in consts],
        out_specs=pl.BlockSpec((tm, d), lambda i: (i, 0)),
        out_shape=jax.ShapeDtypeStruct((n, d), F32),
        compiler_params=_params("parallel"),
        name="merge",
    )(x2, *ys, *consts)


def _ffn_kernel(x_ref, g_ref, w1_ref, w2_ref, gf_ref, out_ref, *, final):
    x = x_ref[...]
    d = x.shape[1]
    h = _rms(x, g_ref[...]).astype(BF16)
    acc = x
    for c in range(w1_ref.shape[1] // d):
        hid = jnp.square(jnp.maximum(_dot(h, w1_ref[:, c * d:(c + 1) * d]), 0.0))
        acc = acc + _dot(hid.astype(BF16), w2_ref[c * d:(c + 1) * d, :])
    if final:
        acc = _rms(acc, gf_ref[...])
    out_ref[...] = acc


def _ffn(x2, g, w1, w2, gf, final):
    n, d = x2.shape
    tm = min(TM_PROJ, n)
    consts = [g, w1, w2, gf]
    return pl.pallas_call(
        functools.partial(_ffn_kernel, final=final),
        grid=(n // tm,),
        in_specs=[pl.BlockSpec((tm, d), lambda i: (i, 0))] + [_const_spec(c.shape) for c in consts],
        out_specs=pl.BlockSpec((tm, d), lambda i: (i, 0)),
        out_shape=jax.ShapeDtypeStruct((n, d), F32),
        compiler_params=_params("parallel"),
        name="ffn",
    )(x2, *consts)


def _w_in_relayout_kernel(w_ref, out_ref):
    w = w_ref[...]
    offs = np.concatenate([[0], np.cumsum(IN_SPLITS)])
    (a_q, a_ckv, a_iq, a_ik, a_iw, b_q, b_k, b_v, b_lr, b_r, c_x, c_y, d_q, d_f, d_i, d_g) = (
        w[:, offs[i]:offs[i + 1]] for i in range(len(IN_SPLITS)))
    pad = None
    parts = [a_q, a_iq, a_iw, pad,
             a_ckv, a_ik, a_ik, a_ik, a_ik,
             b_q, b_k, b_v, b_r, b_lr, pad,
             c_x, c_y, d_q, d_f, d_i, d_g]
    out_ref[...] = jnp.zeros(out_ref.shape, out_ref.dtype)
    o = 0
    for part in parts:
        if part is None:
            o = -(-o // LANES) * LANES
            continue
        out_ref[:, o:o + part.shape[1]] = part.astype(BF16)
        o += part.shape[1]
    assert o == IN_W


def _w_in_relayout(w):
    d, cols = w.shape
    rows = min(128, d)
    return pl.pallas_call(
        _w_in_relayout_kernel,
        grid=(d // rows,),
        in_specs=[pl.BlockSpec((rows, cols), lambda i: (i, 0))],
        out_specs=pl.BlockSpec((rows, IN_W), lambda i: (i, 0)),
        out_shape=jax.ShapeDtypeStruct((d, IN_W), BF16),
        compiler_params=_params("parallel"),
        name="w_in_relayout",
    )(w)


def _block_diag(w):
    n, bi, bj = w.shape
    eye = jnp.eye(n, dtype=w.dtype)
    return (w[:, :, None, :] * eye[:, None, :, None]).reshape(n * bi, n * bj)


def _wide_heads(w):
    h, c, d = w.shape
    eye = jnp.eye(h, dtype=w.dtype)
    return (w[:, :, None, :] * eye[:, None, :, None]).reshape(h, c, h * d)


def kernel(x, norm1_g, w_in, w_gate, b_gate, kv_norm_g, w_uk, w_uv, rel_bias, w_gk2, b_gk, gla_norm_g, conv_w, conv_b, w_rg_a, b_rg_a, w_rg_x, b_rg_x, lru_lambda, lb_param, hgrn_norm_g, w_br_a, w_br_b, w_br_c, w_br_d, w_out, norm2_g, w_ff1, w_ff2, final_norm_g):
    bsz, t, d = x.shape
    depth = w_in.shape[0]
    n = bsz * t
    row = lambda v: v.reshape(1, -1)
    bias_tiles = _bias_tiles(rel_bias)
    x2 = x.reshape(n, d)
    for l in range(depth):
        aq, ak, ckt, pb, pc, pd = _inproj(x2, row(norm1_g[l]), _w_in_relayout(w_in[l]), row(kv_norm_g[l]))
        to3 = lambda a: a.reshape(bsz, t, a.shape[-1])

        y_a = _dsa(to3(aq), to3(ak), ckt, w_uk[l].reshape(A_HEADS * A_HEAD_DIM, A_KV_LATENT).astype(BF16),
                   jnp.swapaxes(_wide_heads(w_uv[l]), 1, 2).astype(BF16), bias_tiles)

        wgk = jnp.concatenate([w_gk2[l], jnp.zeros((LANES - B_GATE_RANK, HEAD_W), F32)], axis=0)
        y_b = _gla_call(_gla_b_kernel, to3(pb),
                        [wgk.astype(BF16), row(b_gk[l]), row(jnp.tile(gla_norm_g[l], B_HEADS))], "gla")

        y_c = _lru(to3(pc), conv_w[l], row(conv_b[l]), _block_diag(w_rg_a[l]).astype(BF16),
                   row(b_rg_a[l]), _block_diag(w_rg_x[l]).astype(BF16), row(b_rg_x[l]),
                   row(lru_lambda[l]))

        y_d = _gla_call(functools.partial(_gla_d_kernel, layer=l), to3(pd),
                        [lb_param, row(jnp.tile(hgrn_norm_g[l], D_HEADS))], "hgrn2")

        ys = [y.reshape(n, HEAD_W) for y in (y_a, y_b, y_c, y_d)]
        wbr = jnp.stack([w_br_a[l], w_br_b[l], w_br_c[l], w_br_d[l]]).astype(BF16)
        x2 = _merge(x2, ys, row(norm1_g[l]), w_gate[l].astype(BF16), row(b_gate[l]), wbr,
                    w_out[l].astype(BF16))
        x2 = _ffn(x2, row(norm2_g[l]), w_ff1[l].astype(BF16), w_ff2[l].astype(BF16),
                  row(final_norm_g), final=(l == depth - 1))
    return x2.reshape(bsz, t, d)
```

```python
import functools
import math

import numpy as np
import jax
import jax.numpy as jnp
from jax import lax
from jax.experimental import pallas as pl
from jax.experimental.pallas import tpu as pltpu

F32 = jnp.float32
BF16 = jnp.bfloat16
I32 = jnp.int32

EPS = 1e-6
NEG_BIG = -1e30
UNSELECTED = -3e38
LB_FLOOR = 1e-20
HI16_MASK = -(2 ** 16)
BF16_MIN_NORMAL_BITS = 0x0080
LOG2E = 1.4426950408889634

A_HEADS = 4
A_HEAD_DIM = 64
A_KV_LATENT = 128
IDX_HEADS = 4
IDX_DIM = 64
MAX_TOPK = 256
Q_BLOCK = 128
Q_STEP = 256
SOFTMAX_DENOM_FLOOR = 2.0 ** -40
N_BUCKETS = 32
MAX_DISTANCE = 128
B_HEADS = 4
B_DK = 64
B_DV = 64
B_GATE_RANK = 16
B_GATE_TAU = 16.0
C_WIDTH = 256
C_BLOCKS = 4
C_CONV = 4
C_EXP = 8.0
D_HEADS = 4
D_DK = 64
D_DV = 64
CHUNK = 64
N_BRANCH = 4

HEAD_W = 256
LANES = 128

IN_SPLITS = (
    A_HEADS * A_HEAD_DIM, A_KV_LATENT, IDX_HEADS * IDX_DIM, IDX_DIM, IDX_HEADS,
    B_HEADS * B_DK, B_HEADS * B_DK, B_HEADS * B_DV, B_GATE_RANK, B_HEADS * B_DV,
    C_WIDTH, C_WIDTH,
    D_HEADS * D_DK, D_HEADS * D_DK, D_HEADS * D_DV, D_HEADS * D_DV,
)

AQ_W = 640
AK_W = 384
PB_W = 1152
PC_W = 512
PD_W = 1024
IN_W = AQ_W + AK_W + PB_W + PC_W + PD_W

VMEM_LIMIT = 56 * 1024 * 1024

TM_PROJ = 512
TT_GLA = 512
TT_LRU = 256
GLA_LEVELS = (32, 16, 8, 4, 2, 1)
GLA_MXU_LEVELS = (2, 1)


def _rms(x, g):
    return x * lax.rsqrt(jnp.mean(x * x, axis=-1, keepdims=True) + EPS) * g


def _dot(a, b):
    return jnp.dot(a, b, preferred_element_type=F32)


def _dot_nt(a, b):
    return lax.dot_general(a, b, (((1,), (1,)), ((), ())), preferred_element_type=F32)


def _split3(x):
    hi = x.astype(BF16)
    rest = x - hi.astype(F32)
    mid = rest.astype(BF16)
    lo = (rest - mid.astype(F32)).astype(BF16)
    return lo, mid, hi


def _stack_heads(a, lane_head):
    return jnp.concatenate([jnp.where(lane_head == h, a, 0.0) for h in range(4)], axis=0)


def _params(*sem):
    return pltpu.CompilerParams(dimension_semantics=sem, vmem_limit_bytes=VMEM_LIMIT)


def _const_spec(shape):
    nd = len(shape)
    return pl.BlockSpec(shape, lambda *_: (0,) * nd)


def _inproj_kernel(x_ref, g_ref, w_ref, kvg_ref, aq_ref, ak_ref, ckt_ref, pb_ref, pc_ref, pd_ref):
    h = _rms(x_ref[...], g_ref[...]).astype(BF16)
    o = 0
    aq_ref[...] = _dot(h, w_ref[:, o:o + AQ_W])
    o += AQ_W
    ak = _dot(h, w_ref[:, o:o + AK_W])
    ckv = _rms(ak[:, :A_KV_LATENT], kvg_ref[...])
    ak_ref[:, :A_KV_LATENT] = ckv.astype(BF16)
    ak_ref[:, A_KV_LATENT:] = ak[:, A_KV_LATENT:].astype(BF16)
    ckt_ref[0] = ckv.T.astype(BF16)
    o += AK_W
    pb_ref[...] = _dot(h, w_ref[:, o:o + PB_W])
    o += PB_W
    pc_ref[...] = _dot(h, w_ref[:, o:o + PC_W])
    o += PC_W
    pd_ref[...] = _dot(h, w_ref[:, o:o + PD_W])


def _inproj(x2, g, w, kvg):
    n, d = x2.shape
    tm = min(TM_PROJ, n)
    def rows(wd, dt):
        return pl.BlockSpec((tm, wd), lambda i: (i, 0)), jax.ShapeDtypeStruct((n, wd), dt)

    ckt = (pl.BlockSpec((1, A_KV_LATENT, tm), lambda i: (i, 0, 0)),
           jax.ShapeDtypeStruct((n // tm, A_KV_LATENT, tm), BF16))
    outs = [rows(AQ_W, F32), rows(AK_W, BF16), ckt, rows(PB_W, F32), rows(PC_W, F32), rows(PD_W, F32)]
    return pl.pallas_call(
        _inproj_kernel,
        grid=(n // tm,),
        in_specs=[pl.BlockSpec((tm, d), lambda i: (i, 0)), _const_spec((1, d)),
                  _const_spec((d, IN_W)), _const_spec((1, A_KV_LATENT))],
        out_specs=[o[0] for o in outs],
        out_shape=[o[1] for o in outs],
        compiler_params=_params("parallel"),
        name="inproj",
    )(x2, g, w, kvg)


def _t5_bucket(dist):
    n = jnp.maximum(dist, 0)
    max_exact = N_BUCKETS // 2
    nf = jnp.maximum(n, max_exact).astype(F32)
    large = max_exact + (jnp.log(nf / max_exact) / math.log(MAX_DISTANCE / max_exact)
                         * (N_BUCKETS - max_exact)).astype(I32)
    large = jnp.minimum(large, N_BUCKETS - 1)
    return jnp.where(n < max_exact, n, large)


def _bias_tiles_kernel(rb_ref, out_ref):
    krow = lax.broadcasted_iota(I32, (Q_BLOCK, Q_BLOCK), 0)
    qcol = lax.broadcasted_iota(I32, (Q_BLOCK, Q_BLOCK), 1)
    for d in range(3):
        bucket = _t5_bucket(d * Q_BLOCK + qcol - krow)
        for h in range(A_HEADS):
            acc = jnp.zeros((Q_BLOCK, Q_BLOCK), F32)
            for k in range(N_BUCKETS):
                acc = jnp.where(bucket == k, rb_ref[k, h], acc)
            out_ref[d, :, h * Q_BLOCK:(h + 1) * Q_BLOCK] = acc


def _bias_tiles(rel_bias):
    return pl.pallas_call(
        _bias_tiles_kernel,
        in_specs=[pl.BlockSpec(memory_space=pltpu.SMEM)],
        out_specs=pl.BlockSpec(memory_space=pltpu.VMEM),
        out_shape=jax.ShapeDtypeStruct((3, Q_BLOCK, A_HEADS * Q_BLOCK), F32),
        name="bias_tiles",
    )(rel_bias)


def _dsa_kernel(aq_ref, ak_ref, ckt_ref, wuk_ref, wuvt_ref, bias_ref, tri_ref, out_ref,
                keys_ref, khi_ref, kmid_ref, s_ref, acc_ref, *, topk, ksb, nu):
    kg = Q_BLOCK
    qw = nu * kg
    ng = ksb // kg
    qi = pl.program_id(1)
    nsb = ((qi + 1) * nu - 1) // ng + 1
    aq = aq_ref[...]
    lane_head = lax.broadcasted_iota(I32, (qw, HEAD_W), 1) // A_HEAD_DIM
    krow = lax.broadcasted_iota(I32, (kg, kg), 0)
    qcol = lax.broadcasted_iota(I32, (kg, kg), 1)

    def causal(sb, g, u):
        return sb * ksb + g * kg + krow <= (qi * nu + u) * kg + qcol

    groups = [(g, slice(g * kg, (g + 1) * kg)) for g in range(ng)]
    heads = [(h, slice(h * qw, (h + 1) * qw)) for h in range(A_HEADS)]
    blocks_u = [(u, slice(u * kg, (u + 1) * kg)) for u in range(nu)]

    def lanes(h, u):
        return slice(h * qw + u * kg, h * qw + (u + 1) * kg)

    def fold_rows(a, op, rows):
        return op(a.reshape(a.shape[0] // rows, rows, a.shape[1]), axis=0)

    qx = _stack_heads(aq[:, 256:512], lane_head).astype(BF16)
    qlat = _dot(_stack_heads(aq[:, 0:256], lane_head).astype(BF16), wuk_ref[...])
    qlat = (qlat * (A_HEAD_DIM ** -0.5)).astype(BF16)
    iw_t = (aq[:, 512:640] * (IDX_HEADS ** -0.5 * IDX_DIM ** -0.5)).T
    iw_rows = [iw_t[h:h + 1, :] for h in range(IDX_HEADS)]

    def score_body(sb, top):
        off = pl.multiple_of(sb * ksb, ksb)
        ik = ak_ref[pl.ds(off, ksb), A_KV_LATENT:]
        ck = ak_ref[pl.ds(off, ksb), :A_KV_LATENT]
        rel = jnp.maximum(_dot_nt(ik, qx), 0.0)
        st = _dot_nt(ck, qlat)
        for g, rs in groups:
            rows = pl.ds(off + g * kg, kg)
            sc = jnp.zeros((kg, qw), F32)
            for h, cs in heads:
                sc = sc + rel[rs, cs] * iw_rows[h]
            peak = {}
            sc = jnp.where(sc == 0.0, 0.0, sc)
            for u, us in blocks_u:
                bits = pltpu.bitcast(jnp.where(causal(sb, g, u), sc[:, us], NEG_BIG), I32)
                keys_ref[rows, us] = bits ^ ((bits >> 31) & 0x7FFFFFFF)
                khi_ref[rows, us] = pltpu.bitcast(bits & HI16_MASK, F32).astype(BF16)
                bias = bias_ref[jnp.clip(qi * nu + u - (sb * ng + g), 0, 2)]
                for h, _ in heads:
                    logit = (st[rs, lanes(h, u)] + bias[:, h * kg:(h + 1) * kg]) * LOG2E
                    s_ref[rows, lanes(h, u)] = logit
                    peak[h, u] = fold_rows(logit, jnp.max, 8)
            top = jnp.maximum(top, jnp.concatenate(
                [peak[h, u] for h, _ in heads for u, _ in blocks_u], axis=1))
        return top

    top = lax.fori_loop(0, nsb, score_body, jnp.full((8, A_HEADS * qw), NEG_BIG, F32))
    m_bound = jnp.max(top, axis=0, keepdims=True)

    acc_rows = 32
    above, below = 512.0, -1.0

    def count_bf(ref, pred_fn):
        def body(sb, acc):
            off = pl.multiple_of(sb * ksb, ksb)
            hit = jnp.where(pred_fn(ref[pl.ds(off, ksb), :]),
                            jnp.ones((), BF16), jnp.zeros((), BF16))
            parts = [hit[r:r + acc_rows] for r in range(0, ksb, acc_rows)]
            while len(parts) > 1:
                parts = [a + b for a, b in zip(parts[::2], parts[1::2])]
            return acc + parts[0]
        acc = lax.fori_loop(0, nsb, body, jnp.zeros((acc_rows, qw), BF16))
        return jnp.sum(acc.astype(F32), axis=0, keepdims=True)

    def hi_as_bf16(value):
        raw = value ^ ((value >> 15) & 0x7FFF)
        return pltpu.bitcast(jnp.left_shift(raw, 16), F32).astype(BF16)

    def hi_bit_body(i, carry):
        theta, cnt_ge = carry
        cand = theta + jnp.left_shift(jnp.int32(1), 15 - i)
        raw = cand ^ ((cand >> 15) & 0x7FFF)
        raw = jnp.where((raw > 0) & (raw < BF16_MIN_NORMAL_BITS), BF16_MIN_NORMAL_BITS, raw)
        cand_bf = pltpu.bitcast(jnp.left_shift(raw, 16), F32).astype(BF16)
        cnt = count_bf(khi_ref, lambda v: v >= cand_bf)
        ok = cnt >= topk
        return jnp.where(ok, cand, theta), jnp.where(ok, cnt, cnt_ge)

    def next_digit(src_ref, dst_ref, theta_bf, shift):
        def body(sb, _):
            rows = pl.ds(pl.multiple_of(sb * ksb, ksb), ksb)
            digit = ((keys_ref[rows, :] >> shift) & 0xFF).astype(F32).astype(BF16)
            src = src_ref[rows, :]
            dst_ref[rows, :] = jnp.where(src > theta_bf, above,
                                         jnp.where(src < theta_bf, below, digit)).astype(BF16)
            return 0
        lax.fori_loop(0, nsb, body, 0)

    def digit_search(ref, cnt_ge):
        def bit_body(i, carry):
            theta, cnt_ge = carry
            cand = theta + jnp.left_shift(jnp.int32(1), 7 - i)
            cand_bf = cand.astype(F32).astype(BF16)
            cnt = count_bf(ref, lambda v: v >= cand_bf)
            ok = cnt >= topk
            return jnp.where(ok, cand, theta), jnp.where(ok, cnt, cnt_ge)
        return lax.fori_loop(0, 8, bit_body, (jnp.zeros((1, qw), I32), cnt_ge))

    init = (jnp.full((1, qw), -(2 ** 15), I32), jnp.zeros((1, qw), F32) + (nsb * ksb).astype(F32))
    theta_hi, cnt_ge = lax.fori_loop(0, 16, hi_bit_body, init)
    next_digit(khi_ref, kmid_ref, hi_as_bf16(theta_hi), 8)
    theta_mid, cnt_ge = digit_search(kmid_ref, cnt_ge)
    next_digit(kmid_ref, khi_ref, theta_mid.astype(F32).astype(BF16), 0)
    theta_lo, cnt_ge = digit_search(khi_ref, cnt_ge)
    theta = theta_hi * 65536 + theta_mid * 256 + theta_lo
    lo_bf = theta_lo.astype(F32).astype(BF16)
    need = topk - count_bf(khi_ref, lambda v: v > lo_bf)
    tie_break = jnp.max(cnt_ge) > topk

    ones = jnp.ones((A_KV_LATENT, ksb), BF16)

    def masked_logits(ranked, sb, tie_seen):
        off = pl.multiple_of(sb * ksb, ksb)
        key = keys_ref[pl.ds(off, ksb), :]
        if ranked:
            eqf = jnp.where(key == theta, 1.0, 0.0)
            rank = tie_seen + _dot(tri_ref[...], eqf.astype(BF16))
            tie_seen = rank[ksb - 1:ksb] + eqf[ksb - 1:ksb]

        def selected(rs, us):
            k, th = key[rs, us], theta[:, us]
            if ranked:
                return (k > th) | ((k == th) & (rank[rs, us] < need[:, us]))
            return k >= th

        out = []
        for g, rs in groups:
            raw = s_ref[pl.ds(off + g * kg, kg), :]
            valid = [selected(rs, us) & causal(sb, g, u) for u, us in blocks_u]
            out.append(jnp.concatenate(
                [jnp.where(valid[u], raw[:, lanes(h, u)], UNSELECTED)
                 for h, _ in heads for u, _ in blocks_u], axis=1))
        return jnp.concatenate(out, axis=0), tie_seen

    def accumulate(sb, p, scale=None):
        new = _dot(jnp.concatenate([ckt_ref[sb], ones], axis=0), p.astype(BF16))
        acc_ref[...] = new + (acc_ref[...] if scale is None else acc_ref[...] * scale)

    def attend_online(ranked):
        acc_ref[...] = jnp.zeros_like(acc_ref)

        def body(sb, carry):
            m, tie_seen = carry
            logits, tie_seen = masked_logits(ranked, sb, tie_seen)
            m_new = jnp.maximum(m, jnp.max(fold_rows(logits, jnp.max, 8), axis=0, keepdims=True))
            accumulate(sb, jnp.exp2(logits - m_new), jnp.exp2(m - m_new))
            return m_new, tie_seen

        lax.fori_loop(0, nsb, body,
                      (jnp.full((1, A_HEADS * qw), NEG_BIG, F32), jnp.zeros((1, qw), F32)))

    def attend_bounded():
        acc_ref[...] = jnp.zeros_like(acc_ref)

        def body(sb, _):
            logits, _ = masked_logits(False, sb, None)
            accumulate(sb, jnp.exp2(logits - m_bound))
            return 0

        lax.fori_loop(0, nsb, body, 0)
        smallest = jnp.min(acc_ref[A_KV_LATENT:A_KV_LATENT + 1, :])

        @pl.when(jnp.logical_not(smallest > SOFTMAX_DENOM_FLOOR))
        def _():
            attend_online(False)

    lax.cond(tie_break, lambda: attend_online(True), attend_bounded)
    acc = acc_ref[...]
    o_t = (acc[:A_KV_LATENT] / acc[A_KV_LATENT:]).astype(BF16)
    y_t = jnp.zeros((HEAD_W, qw), F32)
    for h, cs in heads:
        y_t = y_t + _dot(wuvt_ref[h], o_t[:, cs])
    out_ref[...] = y_t.T


def _dsa(aq, ak, ckt, wuk, wuvt, bias_tiles):
    b, t, _ = aq.shape
    ksb = ckt.shape[2]
    nu = Q_STEP // Q_BLOCK
    assert t % ksb == 0 and ksb % Q_BLOCK == 0 and ckt.shape[0] * ksb == b * t and t % Q_STEP == 0
    nsb = t // ksb
    topk = min(MAX_TOPK, t // 4)
    idx = np.arange(ksb)
    tri = jnp.asarray(idx[None, :] < idx[:, None], BF16)
    return pl.pallas_call(
        functools.partial(_dsa_kernel, topk=topk, ksb=ksb, nu=nu),
        grid=(b, t // Q_STEP),
        in_specs=[pl.BlockSpec((None, Q_STEP, AQ_W), lambda i, j: (i, j, 0)),
                  pl.BlockSpec((None, t, AK_W), lambda i, j: (i, 0, 0)),
                  pl.BlockSpec((nsb, A_KV_LATENT, ksb), lambda i, j: (i, 0, 0)),
                  _const_spec(wuk.shape), _const_spec(wuvt.shape), _const_spec(bias_tiles.shape),
                  _const_spec(tri.shape)],
        out_specs=pl.BlockSpec((None, Q_STEP, HEAD_W), lambda i, j: (i, j, 0)),
        out_shape=jax.ShapeDtypeStruct((b, t, HEAD_W), F32),
        scratch_shapes=[pltpu.VMEM((t, Q_STEP), I32), pltpu.VMEM((t, Q_STEP), BF16),
                        pltpu.VMEM((t, Q_STEP), BF16), pltpu.VMEM((t, A_HEADS * Q_STEP), F32),
                        pltpu.VMEM((2 * A_KV_LATENT, A_HEADS * Q_STEP), F32)],
        compiler_params=_params("parallel", "arbitrary"),
        name="dsa",
    )(aq, ak, ckt, wuk, wuvt, bias_tiles, tri)


def _gla_static_tables():
    c = CHUNK
    idx = np.arange(c)
    tri = (idx[None, :] <= idx[:, None]).astype(np.float32)
    blocks = [tri]
    for s in GLA_MXU_LEVELS:
        end_left = (idx // (2 * s)) * 2 * s + s - 1
        blocks.append(tri[end_left])
    lev = np.full((c, c), -1, np.int32)
    for i in range(c):
        for j in range(c):
            if i == j:
                lev[i, j] = len(GLA_LEVELS)
            elif i > j:
                s = 1 << int(math.floor(math.log2(i ^ j)))
                lev[i, j] = GLA_LEVELS.index(s)
    pool = np.kron(np.eye(4, dtype=np.float32), np.full((64, 64), 1.0 / 64, np.float32))
    return np.concatenate(blocks, axis=0), np.tile(lev, (1, 4)), pool


def _expm1(x):
    return jnp.tanh(0.5 * x) * (jnp.exp(x) + 1.0)


def _log_sigmoid(x):
    return -(jnp.maximum(-x, 0.0) + jnp.log1p(jnp.exp(-jnp.abs(x))))


def _gla_core(q, k, v, g, mall, lev, st):
    c = CHUNK
    lane_head = lax.broadcasted_iota(I32, (c, HEAD_W), 1) // 64
    rowc = lax.broadcasted_iota(I32, (c, HEAD_W), 0)
    bc = sum(_dot(mall, part) for part in _split3(g))
    b = bc[0:c]
    blast = b[c - 1:c, :]
    att = jnp.zeros((c, HEAD_W), F32)
    for li, s in enumerate(GLA_LEVELS):
        if s in GLA_MXU_LEVELS:
            at = GLA_MXU_LEVELS.index(s) + 1
            mid = bc[at * c:(at + 1) * c]
        else:
            mid = jnp.concatenate([jnp.broadcast_to(b[p + s - 1:p + s, :], (2 * s, HEAD_W))
                                   for p in range(0, c, 2 * s)], axis=0)
        right = (rowc & s) != 0
        decay = jnp.exp(-jnp.abs(b - mid))
        qt = jnp.where(right, q * decay, 0.0)
        kt = jnp.where(right, 0.0, k * decay)
        al = _dot_nt(qt.astype(BF16), _stack_heads(kt, lane_head).astype(BF16))
        att = jnp.where(lev == li, al, att)
    ad = _dot_nt(q.astype(BF16), _stack_heads(k, lane_head).astype(BF16))
    att = jnp.where(lev == len(GLA_LEVELS), ad, att)
    o = _dot(att.astype(BF16), _stack_heads(v, lane_head).astype(BF16))
    o = o + _dot_nt((q * jnp.exp(b)).astype(BF16), st.astype(BF16))
    ktail = k * jnp.exp(blast - b)
    upd = _dot(v.T.astype(BF16), ktail.astype(BF16))
    r2 = lax.broadcasted_iota(I32, (HEAD_W, HEAD_W), 0) // 64
    c2 = lax.broadcasted_iota(I32, (HEAD_W, HEAD_W), 1) // 64
    st = st * jnp.exp(blast) + jnp.where(r2 == c2, upd, 0.0)
    return o, st


def _gla_finish(o, r, ng, pool):
    ms = sum(_dot(part, pool) for part in _split3(o * o))
    return o * lax.rsqrt(ms + EPS) * ng * jax.nn.silu(r)


def _gla_b_kernel(p_ref, wgk_ref, bgk_ref, ng_ref, mall_ref, lev_ref, pool_ref, out_ref, st_ref):
    @pl.when(pl.program_id(1) == 0)
    def _():
        st_ref[...] = jnp.zeros_like(st_ref)

    tile = p_ref[...]
    q = tile[:, 0:256] * (B_DK ** -0.5)
    k = tile[:, 256:512]
    v = tile[:, 512:768]
    r = tile[:, 768:1024]
    z = _dot(tile[:, 1024:1152].astype(BF16), wgk_ref[...]) + bgk_ref[...]
    g = _log_sigmoid(z) / B_GATE_TAU
    mall, lev, st = mall_ref[...], lev_ref[...], st_ref[...]
    outs = []
    for c in range(tile.shape[0] // CHUNK):
        sl = slice(c * CHUNK, (c + 1) * CHUNK)
        o, st = _gla_core(q[sl], k[sl], v[sl], g[sl], mall, lev, st)
        outs.append(o)
    st_ref[...] = st
    out_ref[...] = _gla_finish(jnp.concatenate(outs, axis=0), r, ng_ref[...], pool_ref[...])


def _gla_d_kernel(p_ref, lbp_ref, ng_ref, mall_ref, lev_ref, pool_ref, out_ref, st_ref, *, layer):
    @pl.when(pl.program_id(1) == 0)
    def _():
        st_ref[...] = jnp.zeros_like(st_ref)

    lbp = lbp_ref[...]
    e = jnp.exp(lbp - jnp.max(lbp, axis=0, keepdims=True))
    soft = e / jnp.sum(e, axis=0, keepdims=True)
    cum = soft[0:1]
    for i in range(1, layer + 1):
        cum = cum + soft[i:i + 1]
    lb = cum - soft[0:1]

    tile = p_ref[...]
    q = jax.nn.silu(tile[:, 0:256])
    v = tile[:, 512:768]
    r = tile[:, 768:1024]
    t1 = jnp.log(jnp.maximum(lb, LB_FLOOR))
    t2 = jnp.log1p(-lb) + _log_sigmoid(tile[:, 256:512])
    g = jnp.maximum(t1, t2) + jnp.log1p(jnp.exp(-jnp.abs(t1 - t2)))
    k = -_expm1(g)
    mall, lev, st = mall_ref[...], lev_ref[...], st_ref[...]
    outs = []
    for c in range(tile.shape[0] // CHUNK):
        sl = slice(c * CHUNK, (c + 1) * CHUNK)
        o, st = _gla_core(q[sl], k[sl], v[sl], g[sl], mall, lev, st)
        outs.append(o)
    st_ref[...] = st
    out_ref[...] = _gla_finish(jnp.concatenate(outs, axis=0), r, ng_ref[...], pool_ref[...])


def _gla_call(kernel, p, extra, name):
    b, t, w = p.shape
    tt = min(TT_GLA, t)
    mall, lev, pool = _gla_static_tables()
    consts = list(extra) + [jnp.asarray(mall, BF16), jnp.asarray(lev), jnp.asarray(pool, BF16)]
    return pl.pallas_call(
        kernel,
        grid=(b, t // tt),
        in_specs=[pl.BlockSpec((None, tt, w), lambda i, j: (i, j, 0))]
        + [_const_spec(c.shape) for c in consts],
        out_specs=pl.BlockSpec((None, tt, HEAD_W), lambda i, j: (i, j, 0)),
        out_shape=jax.ShapeDtypeStruct((b, t, HEAD_W), F32),
        scratch_shapes=[pltpu.VMEM((HEAD_W, HEAD_W), F32)],
        compiler_params=_params("parallel", "arbitrary"),
        name=name,
    )(p, *consts)


def _lru_kernel(p_ref, cw_ref, cb_ref, wa_ref, ba_ref, wx_ref, bx_ref, lam_ref, out_ref,
                xpad_ref, h_ref):
    tt = p_ref.shape[0]
    pad = 8

    @pl.when(pl.program_id(1) == 0)
    def _():
        xpad_ref[0:pad, :] = jnp.zeros((pad, C_WIDTH), F32)
        h_ref[...] = jnp.zeros_like(h_ref)

    x = p_ref[:, 0:C_WIDTH]
    xpad_ref[pad:pad + tt, :] = x
    base = pad - (C_CONV - 1)
    xc = xpad_ref[base:base + tt, :] * cw_ref[0:1, :]
    for j in range(1, C_CONV):
        xc = xc + xpad_ref[base + j:base + j + tt, :] * cw_ref[j:j + 1, :]
    xc = xc + cb_ref[...]
    xpad_ref[0:pad, :] = x[tt - pad:tt, :]

    xb = xc.astype(BF16)
    r = jax.nn.sigmoid(_dot(xb, wa_ref[...]) + ba_ref[...])
    i = jax.nn.sigmoid(_dot(xb, wx_ref[...]) + bx_ref[...])
    lam = lam_ref[...]
    softplus_neg_lam = jnp.maximum(-lam, 0.0) + jnp.log1p(jnp.exp(-jnp.abs(lam)))
    log_a = -C_EXP * r * softplus_neg_lam
    a = jnp.exp(log_a)
    u = jnp.sqrt(jnp.maximum(-_expm1(2.0 * log_a), 0.0)) * (i * xc)

    rows = lax.broadcasted_iota(I32, (tt, C_WIDTH), 0)
    s = 1
    while s < tt:
        keep = rows >= s
        u = jnp.where(keep, a * pltpu.roll(u, s, 0) + u, u)
        a = jnp.where(keep, a * pltpu.roll(a, s, 0), a)
        s *= 2
    h = u + a * h_ref[0:1, :]
    h_ref[...] = jnp.broadcast_to(h[tt - 1:tt, :], h_ref.shape)
    out_ref[...] = h * jax.nn.gelu(p_ref[:, C_WIDTH:2 * C_WIDTH])


def _lru(pc, cw, cb, wa, ba, wx, bx, lam):
    b, t, w = pc.shape
    tt = min(TT_LRU, t)
    consts = [cw, cb, wa, ba, wx, bx, lam]
    return pl.pallas_call(
        _lru_kernel,
        grid=(b, t // tt),
        in_specs=[pl.BlockSpec((None, tt, w), lambda i, j: (i, j, 0))]
        + [_const_spec(c.shape) for c in consts],
        out_specs=pl.BlockSpec((None, tt, C_WIDTH), lambda i, j: (i, j, 0)),
        out_shape=jax.ShapeDtypeStruct((b, t, C_WIDTH), F32),
        scratch_shapes=[pltpu.VMEM((tt + 8, C_WIDTH), F32), pltpu.VMEM((8, C_WIDTH), F32)],
        compiler_params=_params("parallel", "arbitrary"),
        name="rglru",
    )(pc, *consts)


def _merge_kernel(x_ref, ya_ref, yb_ref, yc_ref, yd_ref, g_ref, wg_ref, bg_ref, wbr_ref, wo_ref,
                  out_ref):
    x = x_ref[...]
    d = x.shape[1]
    h = _rms(x, g_ref[...]).astype(BF16)
    merged = jnp.zeros(x.shape, F32)
    for n, y_ref in enumerate((ya_ref, yb_ref, yc_ref, yd_ref)):
        gate = jax.nn.sigmoid(_dot(h, wg_ref[:, n * d:(n + 1) * d]) + bg_ref[:, n * d:(n + 1) * d])
        merged = merged + gate * _dot(y_ref[...].astype(BF16), wbr_ref[n])
    out_ref[...] = x + _dot(merged.astype(BF16), wo_ref[...])


def _merge(x2, ys, g, wg, bg, wbr, wo):
    n, d = x2.shape
    tm = min(TM_PROJ, n)
    consts = [g, wg, bg, wbr, wo]
    return pl.pallas_call(
        _merge_kernel,
        grid=(n // tm,),
        in_specs=[pl.BlockSpec((tm, d), lambda i: (i, 0))]
        + [pl.BlockSpec((tm, HEAD_W), lambda i: (i, 0)) for _ in ys]
        + [_const_spec(c.shape) for c in consts],
        out_specs=pl.BlockSpec((tm, d), lambda i: (i, 0)),
        out_shape=jax.ShapeDtypeStruct((n, d), F32),
        compiler_params=_params("parallel"),
        name="merge",
    )(x2, *ys, *consts)


def _ffn_kernel(x_ref, g_ref, w1_ref, w2_ref, gf_ref, out_ref, *, final):
    x = x_ref[...]
    d = x.shape[1]
    h = _rms(x, g_ref[...]).astype(BF16)
    acc = x
    for c in range(w1_ref.shape[1] // d):
        hid = jnp.square(jnp.maximum(_dot(h, w1_ref[:, c * d:(c + 1) * d]), 0.0))
        acc = acc + _dot(hid.astype(BF16), w2_ref[c * d:(c + 1) * d, :])
    if final:
        acc = _rms(acc, gf_ref[...])
    out_ref[...] = acc


def _ffn(x2, g, w1, w2, gf, final):
    n, d = x2.shape
    tm = min(TM_PROJ, n)
    consts = [g, w1, w2, gf]
    return pl.pallas_call(
        functools.partial(_ffn_kernel, final=final),
        grid=(n // tm,),
        in_specs=[pl.BlockSpec((tm, d), lambda i: (i, 0))] + [_const_spec(c.shape) for c in consts],
        out_specs=pl.BlockSpec((tm, d), lambda i: (i, 0)),
        out_shape=jax.ShapeDtypeStruct((n, d), F32),
        compiler_params=_params("parallel"),
        name="ffn",
    )(x2, *consts)


def _w_in_relayout_kernel(w_ref, out_ref):
    w = w_ref[...]
    offs = np.concatenate([[0], np.cumsum(IN_SPLITS)])
    (a_q, a_ckv, a_iq, a_ik, a_iw, b_q, b_k, b_v, b_lr, b_r, c_x, c_y, d_q, d_f, d_i, d_g) = (
        w[:, offs[i]:offs[i + 1]] for i in range(len(IN_SPLITS)))
    pad = None
    parts = [a_q, a_iq, a_iw, pad,
             a_ckv, a_ik, a_ik, a_ik, a_ik,
             b_q, b_k, b_v, b_r, b_lr, pad,
             c_x, c_y, d_q, d_f, d_i, d_g]
    out_ref[...] = jnp.zeros(out_ref.shape, out_ref.dtype)
    o = 0
    for part in parts:
        if part is None:
            o = -(-o // LANES) * LANES
            continue
        out_ref[:, o:o + part.shape[1]] = part.astype(BF16)
        o += part.shape[1]
    assert o == IN_W


def _w_in_relayout(w):
    d, cols = w.shape
    rows = min(128, d)
    return pl.pallas_call(
        _w_in_relayout_kernel,
        grid=(d // rows,),
        in_specs=[pl.BlockSpec((rows, cols), lambda i: (i, 0))],
        out_specs=pl.BlockSpec((rows, IN_W), lambda i: (i, 0)),
        out_shape=jax.ShapeDtypeStruct((d, IN_W), BF16),
        compiler_params=_params("parallel"),
        name="w_in_relayout",
    )(w)


def _block_diag(w):
    n, bi, bj = w.shape
    eye = jnp.eye(n, dtype=w.dtype)
    return (w[:, :, None, :] * eye[:, None, :, None]).reshape(n * bi, n * bj)


def _wide_heads(w):
    h, c, d = w.shape
    eye = jnp.eye(h, dtype=w.dtype)
    return (w[:, :, None, :] * eye[:, None, :, None]).reshape(h, c, h * d)


def kernel(x, norm1_g, w_in, w_gate, b_gate, kv_norm_g, w_uk, w_uv, rel_bias, w_gk2, b_gk, gla_norm_g, conv_w, conv_b, w_rg_a, b_rg_a, w_rg_x, b_rg_x, lru_lambda, lb_param, hgrn_norm_g, w_br_a, w_br_b, w_br_c, w_br_d, w_out, norm2_g, w_ff1, w_ff2, final_norm_g):
    bsz, t, d = x.shape
    depth = w_in.shape[0]
    n = bsz * t
    row = lambda v: v.reshape(1, -1)
    bias_tiles = _bias_tiles(rel_bias)
    x2 = x.reshape(n, d)
    for l in range(depth):
        aq, ak, ckt, pb, pc, pd = _inproj(x2, row(norm1_g[l]), _w_in_relayout(w_in[l]), row(kv_norm_g[l]))
        to3 = lambda a: a.reshape(bsz, t, a.shape[-1])

        y_a = _dsa(to3(aq), to3(ak), ckt, w_uk[l].reshape(A_HEADS * A_HEAD_DIM, A_KV_LATENT).astype(BF16),
                   jnp.swapaxes(_wide_heads(w_uv[l]), 1, 2).astype(BF16), bias_tiles)

        wgk = jnp.concatenate([w_gk2[l], jnp.zeros((LANES - B_GATE_RANK, HEAD_W), F32)], axis=0)
        y_b = _gla_call(_gla_b_kernel, to3(pb),
                        [wgk.astype(BF16), row(b_gk[l]), row(jnp.tile(gla_norm_g[l], B_HEADS))], "gla")

        y_c = _lru(to3(pc), conv_w[l], row(conv_b[l]), _block_diag(w_rg_a[l]).astype(BF16),
                   row(b_rg_a[l]), _block_diag(w_rg_x[l]).astype(BF16), row(b_rg_x[l]),
                   row(lru_lambda[l]))

        y_d = _gla_call(functools.partial(_gla_d_kernel, layer=l), to3(pd),
                        [lb_param, row(jnp.tile(hgrn_norm_g[l], D_HEADS))], "hgrn2")

        ys = [y.reshape(n, HEAD_W) for y in (y_a, y_b, y_c, y_d)]
        wbr = jnp.stack([w_br_a[l], w_br_b[l], w_br_c[l], w_br_d[l]]).astype(BF16)
        x2 = _merge(x2, ys, row(norm1_g[l]), w_gate[l].astype(BF16), row(b_gate[l]), wbr,
                    w_out[l].astype(BF16))
        x2 = _ffn(x2, row(norm2_g[l]), w_ff1[l].astype(BF16), w_ff2[l].astype(BF16),
                  row(final_norm_g), final=(l == depth - 1))
    return x2.reshape(bsz, t, d)
```

```python
import functools
import math

import numpy as np
import jax
import jax.numpy as jnp
from jax import lax
from jax.experimental import pallas as pl
from jax.experimental.pallas import tpu as pltpu

F32 = jnp.float32
BF16 = jnp.bfloat16
I32 = jnp.int32

EPS = 1e-6
NEG_BIG = -1e30
UNSELECTED = -3e38
LB_FLOOR = 1e-20
HI16_MASK = -(2 ** 16)
BF16_MIN_NORMAL_BITS = 0x0080
LOG2E = 1.4426950408889634

A_HEADS = 4
A_HEAD_DIM = 64
A_KV_LATENT = 128
IDX_HEADS = 4
IDX_DIM = 64
MAX_TOPK = 256
Q_BLOCK = 128
Q_STEP = 256
SOFTMAX_DENOM_FLOOR = 2.0 ** -40
N_BUCKETS = 32
MAX_DISTANCE = 128
B_HEADS = 4
B_DK = 64
B_DV = 64
B_GATE_RANK = 16
B_GATE_TAU = 16.0
C_WIDTH = 256
C_BLOCKS = 4
C_CONV = 4
C_EXP = 8.0
D_HEADS = 4
D_DK = 64
D_DV = 64
CHUNK = 64
N_BRANCH = 4

HEAD_W = 256
LANES = 128

IN_SPLITS = (
    A_HEADS * A_HEAD_DIM, A_KV_LATENT, IDX_HEADS * IDX_DIM, IDX_DIM, IDX_HEADS,
    B_HEADS * B_DK, B_HEADS * B_DK, B_HEADS * B_DV, B_GATE_RANK, B_HEADS * B_DV,
    C_WIDTH, C_WIDTH,
    D_HEADS * D_DK, D_HEADS * D_DK, D_HEADS * D_DV, D_HEADS * D_DV,
)

AQ_W = 640
AK_W = 384
PB_W = 1152
PC_W = 512
PD_W = 1024
IN_W = AQ_W + AK_W + PB_W + PC_W + PD_W

VMEM_LIMIT = 56 * 1024 * 1024

TM_PROJ = 512
TT_GLA = 512
TT_LRU = 256
GLA_LEVELS = (32, 16, 8, 4, 2, 1)
GLA_MXU_LEVELS = (2, 1)


def _rms(x, g):
    return x * lax.rsqrt(jnp.mean(x * x, axis=-1, keepdims=True) + EPS) * g


def _dot(a, b):
    return jnp.dot(a, b, preferred_element_type=F32)


def _dot_nt(a, b):
    return lax.dot_general(a, b, (((1,), (1,)), ((), ())), preferred_element_type=F32)


def _split3(x):
    hi = x.astype(BF16)
    rest = x - hi.astype(F32)
    mid = rest.astype(BF16)
    lo = (rest - mid.astype(F32)).astype(BF16)
    return lo, mid, hi


def _stack_heads(a, lane_head):
    return jnp.concatenate([jnp.where(lane_head == h, a, 0.0) for h in range(4)], axis=0)


def _params(*sem):
    return pltpu.CompilerParams(dimension_semantics=sem, vmem_limit_bytes=VMEM_LIMIT)


def _const_spec(shape):
    nd = len(shape)
    return pl.BlockSpec(shape, lambda *_: (0,) * nd)


def _inproj_kernel(x_ref, g_ref, w_ref, kvg_ref, aq_ref, ak_ref, ckt_ref, pb_ref, pc_ref, pd_ref):
    h = _rms(x_ref[...], g_ref[...]).astype(BF16)
    o = 0
    aq_ref[...] = _dot(h, w_ref[:, o:o + AQ_W])
    o += AQ_W
    ak = _dot(h, w_ref[:, o:o + AK_W])
    ckv = _rms(ak[:, :A_KV_LATENT], kvg_ref[...])
    ak_ref[:, :A_KV_LATENT] = ckv.astype(BF16)
    ak_ref[:, A_KV_LATENT:] = ak[:, A_KV_LATENT:].astype(BF16)
    ckt_ref[0] = ckv.T.astype(BF16)
    o += AK_W
    pb_ref[...] = _dot(h, w_ref[:, o:o + PB_W])
    o += PB_W
    pc_ref[...] = _dot(h, w_ref[:, o:o + PC_W])
    o += PC_W
    pd_ref[...] = _dot(h, w_ref[:, o:o + PD_W])


def _inproj(x2, g, w, kvg):
    n, d = x2.shape
    tm = min(TM_PROJ, n)
    def rows(wd, dt):
        return pl.BlockSpec((tm, wd), lambda i: (i, 0)), jax.ShapeDtypeStruct((n, wd), dt)

    ckt = (pl.BlockSpec((1, A_KV_LATENT, tm), lambda i: (i, 0, 0)),
           jax.ShapeDtypeStruct((n // tm, A_KV_LATENT, tm), BF16))
    outs = [rows(AQ_W, F32), rows(AK_W, BF16), ckt, rows(PB_W, F32), rows(PC_W, F32), rows(PD_W, F32)]
    return pl.pallas_call(
        _inproj_kernel,
        grid=(n // tm,),
        in_specs=[pl.BlockSpec((tm, d), lambda i: (i, 0)), _const_spec((1, d)),
                  _const_spec((d, IN_W)), _const_spec((1, A_KV_LATENT))],
        out_specs=[o[0] for o in outs],
        out_shape=[o[1] for o in outs],
        compiler_params=_params("parallel"),
        name="inproj",
    )(x2, g, w, kvg)


def _t5_bucket(dist):
    n = jnp.maximum(dist, 0)
    max_exact = N_BUCKETS // 2
    nf = jnp.maximum(n, max_exact).astype(F32)
    large = max_exact + (jnp.log(nf / max_exact) / math.log(MAX_DISTANCE / max_exact)
                         * (N_BUCKETS - max_exact)).astype(I32)
    large = jnp.minimum(large, N_BUCKETS - 1)
    return jnp.where(n < max_exact, n, large)


def _bias_tiles_kernel(rb_ref, out_ref):
    krow = lax.broadcasted_iota(I32, (Q_BLOCK, Q_BLOCK), 0)
    qcol = lax.broadcasted_iota(I32, (Q_BLOCK, Q_BLOCK), 1)
    for d in range(3):
        bucket = _t5_bucket(d * Q_BLOCK + qcol - krow)
        for h in range(A_HEADS):
            acc = jnp.zeros((Q_BLOCK, Q_BLOCK), F32)
            for k in range(N_BUCKETS):
                acc = jnp.where(bucket == k, rb_ref[k, h], acc)
            out_ref[d, :, h * Q_BLOCK:(h + 1) * Q_BLOCK] = acc


def _bias_tiles(rel_bias):
    return pl.pallas_call(
        _bias_tiles_kernel,
        in_specs=[pl.BlockSpec(memory_space=pltpu.SMEM)],
        out_specs=pl.BlockSpec(memory_space=pltpu.VMEM),
        out_shape=jax.ShapeDtypeStruct((3, Q_BLOCK, A_HEADS * Q_BLOCK), F32),
        name="bias_tiles",
    )(rel_bias)


def _dsa_kernel(aq_ref, ak_ref, ckt_ref, wuk_ref, wuvt_ref, bias_ref, tri_ref, out_ref,
                keys_ref, khi_ref, kmid_ref, s_ref, acc_ref, *, topk, ksb, nu):
    kg = Q_BLOCK
    qw = nu * kg
    ng = ksb // kg
    qi = pl.program_id(1)
    nsb = ((qi + 1) * nu - 1) // ng + 1
    aq = aq_ref[...]
    lane_head = lax.broadcasted_iota(I32, (qw, HEAD_W), 1) // A_HEAD_DIM
    krow = lax.broadcasted_iota(I32, (kg, kg), 0)
    qcol = lax.broadcasted_iota(I32, (kg, kg), 1)

    def causal(sb, g, u):
        return sb * ksb + g * kg + krow <= (qi * nu + u) * kg + qcol

    groups = [(g, slice(g * kg, (g + 1) * kg)) for g in range(ng)]
    heads = [(h, slice(h * qw, (h + 1) * qw)) for h in range(A_HEADS)]
    blocks_u = [(u, slice(u * kg, (u + 1) * kg)) for u in range(nu)]

    def lanes(h, u):
        return slice(h * qw + u * kg, h * qw + (u + 1) * kg)

    def fold_rows(a, op, rows):
        return op(a.reshape(a.shape[0] // rows, rows, a.shape[1]), axis=0)

    def loop_steps(step, init):
        def pair(i, carry):
            return step(2 * i + 1, step(2 * i, carry))
        carry = lax.fori_loop(0, lax.shift_right_logical(nsb, 1), pair, init)
        return lax.cond((nsb & 1) == 1, lambda c: step(nsb - 1, c), lambda c: c, carry)

    qx = _stack_heads(aq[:, 256:512], lane_head).astype(BF16)
    qlat = _dot(_stack_heads(aq[:, 0:256], lane_head).astype(BF16), wuk_ref[...])
    qlat = (qlat * (A_HEAD_DIM ** -0.5)).astype(BF16)
    iw_t = (aq[:, 512:640] * (IDX_HEADS ** -0.5 * IDX_DIM ** -0.5)).T
    iw_rows = [iw_t[h:h + 1, :] for h in range(IDX_HEADS)]

    def score_body(sb, top):
        off = pl.multiple_of(sb * ksb, ksb)
        ik = ak_ref[pl.ds(off, ksb), A_KV_LATENT:]
        ck = ak_ref[pl.ds(off, ksb), :A_KV_LATENT]
        rel = jnp.maximum(_dot_nt(ik, qx), 0.0)
        st = _dot_nt(ck, qlat)
        for g, rs in groups:
            rows = pl.ds(off + g * kg, kg)
            sc = jnp.zeros((kg, qw), F32)
            for h, cs in heads:
                sc = sc + rel[rs, cs] * iw_rows[h]
            peak = {}
            sc = jnp.where(sc == 0.0, 0.0, sc)
            for u, us in blocks_u:
                bits = pltpu.bitcast(jnp.where(causal(sb, g, u), sc[:, us], NEG_BIG), I32)
                keys_ref[rows, us] = bits ^ ((bits >> 31) & 0x7FFFFFFF)
                khi_ref[rows, us] = pltpu.bitcast(bits & HI16_MASK, F32).astype(BF16)
                bias = bias_ref[jnp.clip(qi * nu + u - (sb * ng + g), 0, 2)]
                for h, _ in heads:
                    logit = (st[rs, lanes(h, u)] + bias[:, h * kg:(h + 1) * kg]) * LOG2E
                    s_ref[rows, lanes(h, u)] = logit
                    peak[h, u] = fold_rows(logit, jnp.max, 8)
            top = jnp.maximum(top, jnp.concatenate(
                [peak[h, u] for h, _ in heads for u, _ in blocks_u], axis=1))
        return top

    top = loop_steps(score_body, jnp.full((8, A_HEADS * qw), NEG_BIG, F32))
    m_bound = jnp.max(top, axis=0, keepdims=True)

    acc_rows = 32
    above, below = 512.0, -1.0

    def count_bf(ref, pred_fn):
        def body(sb, acc):
            off = pl.multiple_of(sb * ksb, ksb)
            hit = jnp.where(pred_fn(ref[pl.ds(off, ksb), :]),
                            jnp.ones((), BF16), jnp.zeros((), BF16))
            parts = [hit[r:r + acc_rows] for r in range(0, ksb, acc_rows)]
            while len(parts) > 1:
                parts = [a + b for a, b in zip(parts[::2], parts[1::2])]
            return acc + parts[0]
        acc = lax.fori_loop(0, nsb, body, jnp.zeros((acc_rows, qw), BF16))
        return jnp.sum(acc.astype(F32), axis=0, keepdims=True)

    def hi_as_bf16(value):
        raw = value ^ ((value >> 15) & 0x7FFF)
        return pltpu.bitcast(jnp.left_shift(raw, 16), F32).astype(BF16)

    def hi_bit_body(i, carry):
        theta, cnt_ge = carry
        cand = theta + jnp.left_shift(jnp.int32(1), 15 - i)
        raw = cand ^ ((cand >> 15) & 0x7FFF)
        raw = jnp.where((raw > 0) & (raw < BF16_MIN_NORMAL_BITS), BF16_MIN_NORMAL_BITS, raw)
        cand_bf = pltpu.bitcast(jnp.left_shift(raw, 16), F32).astype(BF16)
        cnt = count_bf(khi_ref, lambda v: v >= cand_bf)
        ok = cnt >= topk
        return jnp.where(ok, cand, theta), jnp.where(ok, cnt, cnt_ge)

    def next_digit(src_ref, dst_ref, theta_bf, shift):
        def body(sb, token):
            rows = pl.ds(pl.multiple_of(sb * ksb, ksb), ksb)
            digit = ((keys_ref[rows, :] >> shift) & 0xFF).astype(F32).astype(BF16)
            src = src_ref[rows, :]
            dst_ref[rows, :] = jnp.where(src > theta_bf, above,
                                         jnp.where(src < theta_bf, below, digit)).astype(BF16)
            return token
        lax.fori_loop(0, nsb, body, jnp.int32(0))

    def digit_search(ref, cnt_ge):
        def bit_body(i, carry):
            theta, cnt_ge = carry
            cand = theta + jnp.left_shift(jnp.int32(1), 7 - i)
            cand_bf = cand.astype(F32).astype(BF16)
            cnt = count_bf(ref, lambda v: v >= cand_bf)
            ok = cnt >= topk
            return jnp.where(ok, cand, theta), jnp.where(ok, cnt, cnt_ge)
        return lax.fori_loop(0, 8, bit_body, (jnp.zeros((1, qw), I32), cnt_ge))

    init = (jnp.full((1, qw), -(2 ** 15), I32), jnp.zeros((1, qw), F32) + (nsb * ksb).astype(F32))
    theta_hi, cnt_ge = lax.fori_loop(0, 16, hi_bit_body, init)
    next_digit(khi_ref, kmid_ref, hi_as_bf16(theta_hi), 8)
    theta_mid, cnt_ge = digit_search(kmid_ref, cnt_ge)
    next_digit(kmid_ref, khi_ref, theta_mid.astype(F32).astype(BF16), 0)
    theta_lo, cnt_ge = digit_search(khi_ref, cnt_ge)
    theta = theta_hi * 65536 + theta_mid * 256 + theta_lo
    lo_bf = theta_lo.astype(F32).astype(BF16)
    need = topk - count_bf(khi_ref, lambda v: v > lo_bf)
    tie_break = jnp.max(cnt_ge) > topk

    ones = jnp.ones((A_KV_LATENT, ksb), BF16)

    def masked_logits(ranked, sb, tie_seen):
        off = pl.multiple_of(sb * ksb, ksb)
        key = keys_ref[pl.ds(off, ksb), :]
        if ranked:
            eqf = jnp.where(key == theta, 1.0, 0.0)
            rank = tie_seen + _dot(tri_ref[...], eqf.astype(BF16))
            tie_seen = rank[ksb - 1:ksb] + eqf[ksb - 1:ksb]

        def selected(rs, us):
            k, th = key[rs, us], theta[:, us]
            if ranked:
                return (k > th) | ((k == th) & (rank[rs, us] < need[:, us]))
            return k >= th

        out = []
        for g, rs in groups:
            raw = s_ref[pl.ds(off + g * kg, kg), :]
            valid = [selected(rs, us) & causal(sb, g, u) for u, us in blocks_u]
            out.append(jnp.concatenate(
                [jnp.where(valid[u], raw[:, lanes(h, u)], UNSELECTED)
                 for h, _ in heads for u, _ in blocks_u], axis=1))
        return jnp.concatenate(out, axis=0), tie_seen

    def accumulate(sb, p, scale=None):
        new = _dot(jnp.concatenate([ckt_ref[sb], ones], axis=0), p.astype(BF16))
        acc_ref[...] = new + (acc_ref[...] if scale is None else acc_ref[...] * scale)

    def attend_online(ranked):
        acc_ref[...] = jnp.zeros_like(acc_ref)

        def body(sb, carry):
            m, tie_seen = carry
            logits, tie_seen = masked_logits(ranked, sb, tie_seen)
            m_new = jnp.maximum(m, jnp.max(fold_rows(logits, jnp.max, 8), axis=0, keepdims=True))
            accumulate(sb, jnp.exp2(logits - m_new), jnp.exp2(m - m_new))
            return m_new, tie_seen

        lax.fori_loop(0, nsb, body,
                      (jnp.full((1, A_HEADS * qw), NEG_BIG, F32), jnp.zeros((1, qw), F32)))

    def attend_bounded():
        acc_ref[...] = jnp.zeros_like(acc_ref)

        def body(sb, token):
            logits, _ = masked_logits(False, sb, None)
            accumulate(sb, jnp.exp2(logits - m_bound))
            return token

        loop_steps(body, jnp.int32(0))
        smallest = jnp.min(acc_ref[A_KV_LATENT:A_KV_LATENT + 1, :])

        @pl.when(jnp.logical_not(smallest > SOFTMAX_DENOM_FLOOR))
        def _():
            attend_online(False)

    lax.cond(tie_break, lambda: attend_online(True), attend_bounded)
    acc = acc_ref[...]
    o_t = (acc[:A_KV_LATENT] / acc[A_KV_LATENT:]).astype(BF16)
    y_t = jnp.zeros((HEAD_W, qw), F32)
    for h, cs in heads:
        y_t = y_t + _dot(wuvt_ref[h], o_t[:, cs])
    out_ref[...] = y_t.T


def _dsa(aq, ak, ckt, wuk, wuvt, bias_tiles):
    b, t, _ = aq.shape
    ksb = ckt.shape[2]
    nu = Q_STEP // Q_BLOCK
    assert t % ksb == 0 and ksb % Q_BLOCK == 0 and ckt.shape[0] * ksb == b * t and t % Q_STEP == 0
    nsb = t // ksb
    topk = min(MAX_TOPK, t // 4)
    idx = np.arange(ksb)
    tri = jnp.asarray(idx[None, :] < idx[:, None], BF16)
    return pl.pallas_call(
        functools.partial(_dsa_kernel, topk=topk, ksb=ksb, nu=nu),
        grid=(b, t // Q_STEP),
        in_specs=[pl.BlockSpec((None, Q_STEP, AQ_W), lambda i, j: (i, j, 0)),
                  pl.BlockSpec((None, t, AK_W), lambda i, j: (i, 0, 0)),
                  pl.BlockSpec((nsb, A_KV_LATENT, ksb), lambda i, j: (i, 0, 0)),
                  _const_spec(wuk.shape), _const_spec(wuvt.shape), _const_spec(bias_tiles.shape),
                  _const_spec(tri.shape)],
        out_specs=pl.BlockSpec((None, Q_STEP, HEAD_W), lambda i, j: (i, j, 0)),
        out_shape=jax.ShapeDtypeStruct((b, t, HEAD_W), F32),
        scratch_shapes=[pltpu.VMEM((t, Q_STEP), I32), pltpu.VMEM((t, Q_STEP), BF16),
                        pltpu.VMEM((t, Q_STEP), BF16), pltpu.VMEM((t, A_HEADS * Q_STEP), F32),
                        pltpu.VMEM((2 * A_KV_LATENT, A_HEADS * Q_STEP), F32)],
        compiler_params=_params("parallel", "arbitrary"),
        name="dsa",
    )(aq, ak, ckt, wuk, wuvt, bias_tiles, tri)


def _gla_static_tables():
    c = CHUNK
    idx = np.arange(c)
    tri = (idx[None, :] <= idx[:, None]).astype(np.float32)
    blocks = [tri]
    for s in GLA_MXU_LEVELS:
        end_left = (idx // (2 * s)) * 2 * s + s - 1
        blocks.append(tri[end_left])
    lev = np.full((c, c), -1, np.int32)
    for i in range(c):
        for j in range(c):
            if i == j:
                lev[i, j] = len(GLA_LEVELS)
            elif i > j:
                s = 1 << int(math.floor(math.log2(i ^ j)))
                lev[i, j] = GLA_LEVELS.index(s)
    pool = np.kron(np.eye(4, dtype=np.float32), np.full((64, 64), 1.0 / 64, np.float32))
    return np.concatenate(blocks, axis=0), np.tile(lev, (1, 4)), pool


def _expm1(x):
    return jnp.tanh(0.5 * x) * (jnp.exp(x) + 1.0)


def _log_sigmoid(x):
    return -(jnp.maximum(-x, 0.0) + jnp.log1p(jnp.exp(-jnp.abs(x))))


def _gla_core(q, k, v, g, mall, lev, st):
    c = CHUNK
    rowc = lax.broadcasted_iota(I32, (c, HEAD_W), 0)
    pairs = [slice(p * LANES, (p + 1) * LANES) for p in range(HEAD_W // LANES)]
    first = lax.broadcasted_iota(I32, (c, LANES), 1) < 64

    def stack_pair(a):
        return jnp.concatenate([jnp.where(first, a, 0.0), jnp.where(first, 0.0, a)], axis=0)

    def per_pair(fn):
        return jnp.concatenate([fn(p, ps) for p, ps in enumerate(pairs)], axis=1)

    bc = sum(_dot(mall, part) for part in _split3(g))
    b = bc[0:c]
    blast = b[c - 1:c, :]
    att = jnp.zeros((c, HEAD_W), F32)
    for li, s in enumerate(GLA_LEVELS):
        if s in GLA_MXU_LEVELS:
            at = GLA_MXU_LEVELS.index(s) + 1
            mid = bc[at * c:(at + 1) * c]
        else:
            mid = jnp.concatenate([jnp.broadcast_to(b[p + s - 1:p + s, :], (2 * s, HEAD_W))
                                   for p in range(0, c, 2 * s)], axis=0)
        right = (rowc & s) != 0
        decay = jnp.exp(-jnp.abs(b - mid))
        qt = jnp.where(right, q * decay, 0.0).astype(BF16)
        kt = jnp.where(right, 0.0, k * decay)
        al = per_pair(lambda p, ps: _dot_nt(qt[:, ps], stack_pair(kt[:, ps]).astype(BF16)))
        att = jnp.where(lev == li, al, att)
    qb = q.astype(BF16)
    ad = per_pair(lambda p, ps: _dot_nt(qb[:, ps], stack_pair(k[:, ps]).astype(BF16)))
    att = jnp.where(lev == len(GLA_LEVELS), ad, att).astype(BF16)
    qe = (q * jnp.exp(b)).astype(BF16)
    o = per_pair(lambda p, ps: _dot(att[:, ps], stack_pair(v[:, ps]).astype(BF16))
                 + _dot_nt(qe[:, ps], st[p].astype(BF16)))
    ktail = (k * jnp.exp(blast - b)).astype(BF16)
    r2 = lax.broadcasted_iota(I32, (LANES, LANES), 0) // 64
    c2 = lax.broadcasted_iota(I32, (LANES, LANES), 1) // 64
    new_st = []
    for p, ps in enumerate(pairs):
        upd = _dot(v[:, ps].T.astype(BF16), ktail[:, ps])
        new_st.append(st[p] * jnp.exp(blast[:, ps]) + jnp.where(r2 == c2, upd, 0.0))
    return o, new_st


def _gla_finish(o, r, ng, pool):
    ms = sum(_dot(part, pool) for part in _split3(o * o))
    return o * lax.rsqrt(ms + EPS) * ng * jax.nn.silu(r)


def _gla_b_kernel(p_ref, wgk_ref, bgk_ref, ng_ref, mall_ref, lev_ref, pool_ref, out_ref, st_ref):
    @pl.when(pl.program_id(1) == 0)
    def _():
        st_ref[...] = jnp.zeros_like(st_ref)

    tile = p_ref[...]
    q = tile[:, 0:256] * (B_DK ** -0.5)
    k = tile[:, 256:512]
    v = tile[:, 512:768]
    r = tile[:, 768:1024]
    z = _dot(tile[:, 1024:1152].astype(BF16), wgk_ref[...]) + bgk_ref[...]
    g = _log_sigmoid(z) / B_GATE_TAU
    mall, lev = mall_ref[...], lev_ref[...]
    st = [st_ref[p] for p in range(st_ref.shape[0])]
    outs = []
    for c in range(tile.shape[0] // CHUNK):
        sl = slice(c * CHUNK, (c + 1) * CHUNK)
        o, st = _gla_core(q[sl], k[sl], v[sl], g[sl], mall, lev, st)
        outs.append(o)
    for p, block in enumerate(st):
        st_ref[p] = block
    out_ref[...] = _gla_finish(jnp.concatenate(outs, axis=0), r, ng_ref[...], pool_ref[...])


def _gla_d_kernel(p_ref, lbp_ref, ng_ref, mall_ref, lev_ref, pool_ref, out_ref, st_ref, *, layer):
    @pl.when(pl.program_id(1) == 0)
    def _():
        st_ref[...] = jnp.zeros_like(st_ref)

    lbp = lbp_ref[...]
    e = jnp.exp(lbp - jnp.max(lbp, axis=0, keepdims=True))
    soft = e / jnp.sum(e, axis=0, keepdims=True)
    cum = soft[0:1]
    for i in range(1, layer + 1):
        cum = cum + soft[i:i + 1]
    lb = cum - soft[0:1]

    tile = p_ref[...]
    q = jax.nn.silu(tile[:, 0:256])
    v = tile[:, 512:768]
    r = tile[:, 768:1024]
    t1 = jnp.log(jnp.maximum(lb, LB_FLOOR))
    t2 = jnp.log1p(-lb) + _log_sigmoid(tile[:, 256:512])
    g = jnp.maximum(t1, t2) + jnp.log1p(jnp.exp(-jnp.abs(t1 - t2)))
    k = -_expm1(g)
    mall, lev = mall_ref[...], lev_ref[...]
    st = [st_ref[p] for p in range(st_ref.shape[0])]
    outs = []
    for c in range(tile.shape[0] // CHUNK):
        sl = slice(c * CHUNK, (c + 1) * CHUNK)
        o, st = _gla_core(q[sl], k[sl], v[sl], g[sl], mall, lev, st)
        outs.append(o)
    for p, block in enumerate(st):
        st_ref[p] = block
    out_ref[...] = _gla_finish(jnp.concatenate(outs, axis=0), r, ng_ref[...], pool_ref[...])


def _gla_call(kernel, p, extra, name):
    b, t, w = p.shape
    tt = min(TT_GLA, t)
    mall, lev, pool = _gla_static_tables()
    consts = list(extra) + [jnp.asarray(mall, BF16), jnp.asarray(lev), jnp.asarray(pool, BF16)]
    return pl.pallas_call(
        kernel,
        grid=(b, t // tt),
        in_specs=[pl.BlockSpec((None, tt, w), lambda i, j: (i, j, 0))]
        + [_const_spec(c.shape) for c in consts],
        out_specs=pl.BlockSpec((None, tt, HEAD_W), lambda i, j: (i, j, 0)),
        out_shape=jax.ShapeDtypeStruct((b, t, HEAD_W), F32),
        scratch_shapes=[pltpu.VMEM((HEAD_W // LANES, LANES, LANES), F32)],
        compiler_params=_params("parallel", "arbitrary"),
        name=name,
    )(p, *consts)


def _lru_kernel(p_ref, cw_ref, cb_ref, wa_ref, ba_ref, wx_ref, bx_ref, lam_ref, out_ref,
                xpad_ref, h_ref):
    tt = p_ref.shape[0]
    pad = 8

    @pl.when(pl.program_id(1) == 0)
    def _():
        xpad_ref[0:pad, :] = jnp.zeros((pad, C_WIDTH), F32)
        h_ref[...] = jnp.zeros_like(h_ref)

    x = p_ref[:, 0:C_WIDTH]
    xpad_ref[pad:pad + tt, :] = x
    base = pad - (C_CONV - 1)
    xc = xpad_ref[base:base + tt, :] * cw_ref[0:1, :]
    for j in range(1, C_CONV):
        xc = xc + xpad_ref[base + j:base + j + tt, :] * cw_ref[j:j + 1, :]
    xc = xc + cb_ref[...]
    xpad_ref[0:pad, :] = x[tt - pad:tt, :]

    xb = xc.astype(BF16)
    r = jax.nn.sigmoid(_dot(xb, wa_ref[...]) + ba_ref[...])
    i = jax.nn.sigmoid(_dot(xb, wx_ref[...]) + bx_ref[...])
    lam = lam_ref[...]
    softplus_neg_lam = jnp.maximum(-lam, 0.0) + jnp.log1p(jnp.exp(-jnp.abs(lam)))
    log_a = -C_EXP * r * softplus_neg_lam
    a = jnp.exp(log_a)
    u = jnp.sqrt(jnp.maximum(-_expm1(2.0 * log_a), 0.0)) * (i * xc)

    rows = lax.broadcasted_iota(I32, (tt, C_WIDTH), 0)
    s = 1
    while s < tt:
        keep = rows >= s
        u = jnp.where(keep, a * pltpu.roll(u, s, 0) + u, u)
        a = jnp.where(keep, a * pltpu.roll(a, s, 0), a)
        s *= 2
    h = u + a * h_ref[0:1, :]
    h_ref[...] = jnp.broadcast_to(h[tt - 1:tt, :], h_ref.shape)
    out_ref[...] = h * jax.nn.gelu(p_ref[:, C_WIDTH:2 * C_WIDTH])


def _lru(pc, cw, cb, wa, ba, wx, bx, lam):
    b, t, w = pc.shape
    tt = min(TT_LRU, t)
    consts = [cw, cb, wa, ba, wx, bx, lam]
    return pl.pallas_call(
        _lru_kernel,
        grid=(b, t // tt),
        in_specs=[pl.BlockSpec((None, tt, w), lambda i, j: (i, j, 0))]
        + [_const_spec(c.shape) for c in consts],
        out_specs=pl.BlockSpec((None, tt, C_WIDTH), lambda i, j: (i, j, 0)),
        out_shape=jax.ShapeDtypeStruct((b, t, C_WIDTH), F32),
        scratch_shapes=[pltpu.VMEM((tt + 8, C_WIDTH), F32), pltpu.VMEM((8, C_WIDTH), F32)],
        compiler_params=_params("parallel", "arbitrary"),
        name="rglru",
    )(pc, *consts)


def _merge_kernel(x_ref, ya_ref, yb_ref, yc_ref, yd_ref, g_ref, wg_ref, bg_ref, wbr_ref, wo_ref,
                  out_ref):
    x = x_ref[...]
    d = x.shape[1]
    h = _rms(x, g_ref[...]).astype(BF16)
    merged = jnp.zeros(x.shape, F32)
    for n, y_ref in enumerate((ya_ref, yb_ref, yc_ref, yd_ref)):
        gate = jax.nn.sigmoid(_dot(h, wg_ref[:, n * d:(n + 1) * d]) + bg_ref[:, n * d:(n + 1) * d])
        merged = merged + gate * _dot(y_ref[...].astype(BF16), wbr_ref[n])
    out_ref[...] = x + _dot(merged.astype(BF16), wo_ref[...])


def _merge(x2, ys, g, wg, bg, wbr, wo):
    n, d = x2.shape
    tm = min(TM_PROJ, n)
    consts = [g, wg, bg, wbr, wo]
    return pl.pallas_call(
        _merge_kernel,
        grid=(n // tm,),
        in_specs=[pl.BlockSpec((tm, d), lambda i: (i, 0))]
        + [pl.BlockSpec((tm, HEAD_W), lambda i: (i, 0)) for _ in ys]
        + [_const_spec(c.shape) for c in consts],
        out_specs=pl.BlockSpec((tm, d), lambda i: (i, 0)),
        out_shape=jax.ShapeDtypeStruct((n, d), F32),
        compiler_params=_params("parallel"),
        name="merge",
    )(x2, *ys, *consts)


def _ffn_kernel(x_ref, g_ref, w1_ref, w2_ref, gf_ref, out_ref, *, final):
    x = x_ref[...]
    d = x.shape[1]
    h = _rms(x, g_ref[...]).astype(BF16)
    acc = x
    for c in range(w1_ref.shape[1] // d):
        hid = jnp.square(jnp.maximum(_dot(h, w1_ref[:, c * d:(c + 1) * d]), 0.0))
        acc = acc + _dot(hid.astype(BF16), w2_ref[c * d:(c + 1) * d, :])
    if final:
        acc = _rms(acc, gf_ref[...])
    out_ref[...] = acc


def _ffn(x2, g, w1, w2, gf, final):
    n, d = x2.shape
    tm = min(TM_PROJ, n)
    consts = [g, w1, w2, gf]
    return pl.pallas_call(
        functools.partial(_ffn_kernel, final=final),
        grid=(n // tm,),
        in_specs=[pl.BlockSpec((tm, d), lambda i: (i, 0))] + [_const_spec(c.shape) for c in consts],
        out_specs=pl.BlockSpec((tm, d), lambda i: (i, 0)),
        out_shape=jax.ShapeDtypeStruct((n, d), F32),
        compiler_params=_params("parallel"),
        name="ffn",
    )(x2, *consts)


def _w_in_relayout_kernel(w_ref, out_ref):
    w = w_ref[...]
    offs = np.concatenate([[0], np.cumsum(IN_SPLITS)])
    (a_q, a_ckv, a_iq, a_ik, a_iw, b_q, b_k, b_v, b_lr, b_r, c_x, c_y, d_q, d_f, d_i, d_g) = (
        w[:, offs[i]:offs[i + 1]] for i in range(len(IN_SPLITS)))
    pad = None
    parts = [a_q, a_iq, a_iw, pad,
             a_ckv, a_ik, a_ik, a_ik, a_ik,
             b_q, b_k, b_v, b_r, b_lr, pad,
             c_x, c_y, d_q, d_f, d_i, d_g]
    out_ref[...] = jnp.zeros(out_ref.shape, out_ref.dtype)
    o = 0
    for part in parts:
        if part is None:
            o = -(-o // LANES) * LANES
            continue
        out_ref[:, o:o + part.shape[1]] = part.astype(BF16)
        o += part.shape[1]
    assert o == IN_W


def _w_in_relayout(w):
    d, cols = w.shape
    rows = min(128, d)
    return pl.pallas_call(
        _w_in_relayout_kernel,
        grid=(d // rows,),
        in_specs=[pl.BlockSpec((rows, cols), lambda i: (i, 0))],
        out_specs=pl.BlockSpec((rows, IN_W), lambda i: (i, 0)),
        out_shape=jax.ShapeDtypeStruct((d, IN_W), BF16),
        compiler_params=_params("parallel"),
        name="w_in_relayout",
    )(w)


def _block_diag(w):
    n, bi, bj = w.shape
    eye = jnp.eye(n, dtype=w.dtype)
    return (w[:, :, None, :] * eye[:, None, :, None]).reshape(n * bi, n * bj)


def _wide_heads(w):
    h, c, d = w.shape
    eye = jnp.eye(h, dtype=w.dtype)
    return (w[:, :, None, :] * eye[:, None, :, None]).reshape(h, c, h * d)


def kernel(x, norm1_g, w_in, w_gate, b_gate, kv_norm_g, w_uk, w_uv, rel_bias, w_gk2, b_gk, gla_norm_g, conv_w, conv_b, w_rg_a, b_rg_a, w_rg_x, b_rg_x, lru_lambda, lb_param, hgrn_norm_g, w_br_a, w_br_b, w_br_c, w_br_d, w_out, norm2_g, w_ff1, w_ff2, final_norm_g):
    bsz, t, d = x.shape
    depth = w_in.shape[0]
    n = bsz * t
    row = lambda v: v.reshape(1, -1)
    bias_tiles = _bias_tiles(rel_bias)
    x2 = x.reshape(n, d)
    for l in range(depth):
        aq, ak, ckt, pb, pc, pd = _inproj(x2, row(norm1_g[l]), _w_in_relayout(w_in[l]), row(kv_norm_g[l]))
        to3 = lambda a: a.reshape(bsz, t, a.shape[-1])

        y_a = _dsa(to3(aq), to3(ak), ckt, w_uk[l].reshape(A_HEADS * A_HEAD_DIM, A_KV_LATENT).astype(BF16),
                   jnp.swapaxes(_wide_heads(w_uv[l]), 1, 2).astype(BF16), bias_tiles)

        wgk = jnp.concatenate([w_gk2[l], jnp.zeros((LANES - B_GATE_RANK, HEAD_W), F32)], axis=0)
        y_b = _gla_call(_gla_b_kernel, to3(pb),
                        [wgk.astype(BF16), row(b_gk[l]), row(jnp.tile(gla_norm_g[l], B_HEADS))], "gla")

        y_c = _lru(to3(pc), conv_w[l], row(conv_b[l]), _block_diag(w_rg_a[l]).astype(BF16),
                   row(b_rg_a[l]), _block_diag(w_rg_x[l]).astype(BF16), row(b_rg_x[l]),
                   row(lru_lambda[l]))

        y_d = _gla_call(functools.partial(_gla_d_kernel, layer=l), to3(pd),
                        [lb_param, row(jnp.tile(hgrn_norm_g[l], D_HEADS))], "hgrn2")

        ys = [y.reshape(n, HEAD_W) for y in (y_a, y_b, y_c, y_d)]
        wbr = jnp.stack([w_br_a[l], w_br_b[l], w_br_c[l], w_br_d[l]]).astype(BF16)
        x2 = _merge(x2, ys, row(norm1_g[l]), w_gate[l].astype(BF16), row(b_gate[l]), wbr,
                    w_out[l].astype(BF16))
        x2 = _ffn(x2, row(norm2_g[l]), w_ff1[l].astype(BF16), w_ff2[l].astype(BF16),
                  row(final_norm_g), final=(l == depth - 1))
    return x2.reshape(bsz, t, d)
```
